```python
import jax, jax.numpy as jnp
from jax import lax
import numpy as np

D_MODEL = 1024
BATCH = 8
SEQ = 4096
DEPTH = 1

N_META = 16
BLOCK = 128
N_PAD = BLOCK - N_META
HG_HEADS = 4
HG_DK = 128
HG_DV = 128
HG_WIDTH = HG_HEADS * HG_DK
AT_QHEADS = 8
AT_KVHEADS = 2
AT_GROUP = AT_QHEADS // AT_KVHEADS
AT_HDIM = 64
AT_WIDTH = AT_QHEADS * AT_HDIM
AT_KV_WIDTH = AT_KVHEADS * AT_HDIM
WINDOW = 128
IN_COLS = 4 * HG_WIDTH + AT_WIDTH + 2 * AT_KV_WIDTH + 2 * D_MODEL
PK_HEADS = 8
N_KEYS = 128
N_EXPERTS = N_KEYS * N_KEYS
D_QUERY = 256
D_HALF = D_QUERY // 2
TOPK = 16
TOK_BLOCK = 256
EPS = 1e-6

kernel_name = "hybrid_hgrn2_swa_sink_peer_block"


def _rmsnorm(x, g):
    xf = x.astype(jnp.float32)
    y = xf * lax.rsqrt(jnp.mean(xf * xf, axis=-1, keepdims=True) + EPS)
    return (y * g.astype(jnp.float32)).astype(x.dtype)


def _hgrn2(q, f_logit, v, lb):
    B, L = q.shape[:2]
    lb = lb.reshape(HG_HEADS, HG_DK).astype(jnp.float32)
    f = lb + (1.0 - lb) * jax.nn.sigmoid(f_logit.astype(jnp.float32))
    logf = jnp.log(f)
    k = 1.0 - f
    pad = ((0, 0), (N_PAD, 0), (0, 0), (0, 0))
    qf = jnp.pad(q.astype(jnp.float32), pad)
    logf = jnp.pad(logf, pad)
    k = jnp.pad(k, pad)
    vf = jnp.pad(v.astype(jnp.float32), pad)
    nc = (L + N_PAD) // BLOCK

    def to_chunks(t):
        return t.reshape(B, nc, BLOCK, HG_HEADS, t.shape[-1]).transpose(1, 0, 3, 2, 4)

    causal = jnp.tril(jnp.ones((BLOCK, BLOCK), dtype=bool))[:, :, None]

    def step(S, inp):
        qc, lc, kc, vc = inp
        A = jnp.cumsum(lc, axis=2)
        diff = A[:, :, :, None, :] - A[:, :, None, :, :]
        decay = jnp.where(causal, jnp.exp(jnp.where(causal, diff, 0.0)), 0.0)
        scores = jnp.einsum('bhtsd,bhsd->bhts', qc[:, :, :, None, :] * decay, kc)
        o = (jnp.einsum('bhts,bhse->bhte', scores, vc)
             + jnp.einsum('bhtd,bhde->bhte', qc * jnp.exp(A), S))
        A_end = A[:, :, -1:, :]
        S = (jnp.exp(A_end[:, :, 0, :])[..., None] * S
             + jnp.einsum('bhsd,bhse->bhde', kc * jnp.exp(A_end - A), vc))
        return S, o

    S0 = jnp.zeros((B, HG_HEADS, HG_DK, HG_DV), jnp.float32)
    _, o = lax.scan(step, S0, (to_chunks(qf), to_chunks(logf), to_chunks(k), to_chunks(vf)))
    o = o.transpose(1, 0, 3, 2, 4).reshape(B, nc * BLOCK, HG_HEADS, HG_DV)
    return o[:, N_PAD:]


def _swa_sinks(q, k, v, sinks):
    B, L = q.shape[:2]
    P = L + N_PAD
    nb = P // BLOCK
    pad = ((0, 0), (N_PAD, 0), (0, 0), (0, 0))
    qb = jnp.pad(q, pad).reshape(B, nb, BLOCK, AT_KVHEADS, AT_GROUP, AT_HDIM)
    kp = jnp.pad(k, pad).reshape(B, nb, BLOCK, AT_KVHEADS, AT_HDIM)
    vp = jnp.pad(v, pad).reshape(B, nb, BLOCK, AT_KVHEADS, AT_HDIM)

    def band(t):
        prev = jnp.pad(t, ((0, 0), (1, 0), (0, 0), (0, 0), (0, 0)))[:, :-1]
        return jnp.concatenate([prev, t], axis=2)

    kb, vb = band(kp), band(vp)
    km, vm = k[:, :N_META], v[:, :N_META]
    scale = AT_HDIM ** -0.5
    s_band = jnp.einsum('bnqhgd,bnshd->bnhgqs', qb, kb).astype(jnp.float32) * scale
    s_meta = jnp.einsum('bnqhgd,bmhd->bnhgqm', qb, km).astype(jnp.float32) * scale
    qpos = jnp.arange(nb * BLOCK).reshape(nb, BLOCK) - N_PAD
    kpos = qpos[:, :1] - BLOCK + jnp.arange(2 * BLOCK)[None, :]
    band_mask = ((kpos[:, None, :] >= N_META)
                 & (kpos[:, None, :] <= qpos[:, :, None])
                 & (qpos[:, :, None] - kpos[:, None, :] < WINDOW))
    meta_mask = jnp.arange(N_META)[None, None, :] <= qpos[:, :, None]
    neg = jnp.finfo(jnp.float32).min
    s_band = jnp.where(band_mask[None, :, None, None], s_band, neg)
    s_meta = jnp.where(meta_mask[None, :, None, None], s_meta, neg)
    s_sink = jnp.broadcast_to(
        sinks.reshape(AT_KVHEADS, AT_GROUP).astype(jnp.float32)[None, None, :, :, None, None],
        s_band.shape[:-1] + (1,))
    p = jax.nn.softmax(jnp.concatenate([s_band, s_meta, s_sink], axis=-1), axis=-1)
    p_band = p[..., :2 * BLOCK].astype(v.dtype)
    p_meta = p[..., 2 * BLOCK:2 * BLOCK + N_META].astype(v.dtype)
    o = (jnp.einsum('bnhgqs,bnshd->bnqhgd', p_band, vb)
         + jnp.einsum('bnhgqm,bmhd->bnqhgd', p_meta, vm))
    return o.reshape(B, P, AT_WIDTH)[:, N_PAD:]


def _peer(h, w_q, subkeys, u, v):
    B, L, D = h.shape
    T = B * L
    Tp = -(-T // TOK_BLOCK) * TOK_BLOCK
    t = jnp.pad(h.reshape(T, D), ((0, Tp - T), (0, 0))).reshape(Tp // TOK_BLOCK, TOK_BLOCK, D)

    def block(xb):
        q = (xb @ w_q).reshape(TOK_BLOCK, PK_HEADS, 2, D_HALF)
        s = jnp.einsum('thcd,hcnd->thcn', q, subkeys).astype(jnp.float32)
        sv, si = lax.top_k(s, TOPK)
        cand = sv[:, :, 0, :, None] + sv[:, :, 1, None, :]
        cand_idx = si[:, :, 0, :, None] * N_KEYS + si[:, :, 1, None, :]
        top_s, pos = lax.top_k(cand.reshape(TOK_BLOCK, PK_HEADS, TOPK * TOPK), TOPK)
        idx = jnp.take_along_axis(cand_idx.reshape(TOK_BLOCK, PK_HEADS, TOPK * TOPK), pos, axis=-1)
        g = jax.nn.softmax(top_s, axis=-1)
        a = jax.nn.gelu(jnp.einsum('td,thkd->thk', xb, u[idx]).astype(jnp.float32), approximate=False)
        w = (g * a).astype(xb.dtype)
        return jnp.einsum('thk,thkd->td', w, v[idx])

    out = lax.map(block, t)
    return out.reshape(Tp, D)[:T].reshape(B, L, D)


def setup_inputs(seed: int = 0) -> dict:
    key = jax.random.key(seed)
    ks = jax.random.split(key, 17)
    f32 = jnp.float32
    nrm = lambda k, shape, s: jax.random.normal(k, shape, f32) * s
    return {
        "x": nrm(ks[0], (BATCH, SEQ, D_MODEL), 1.0),
        "meta_tokens": nrm(ks[1], (N_META, D_MODEL), 1.0),
        "mix_norm_g": 1.0 + nrm(ks[2], (DEPTH, D_MODEL), 0.02),
        "w_in": nrm(ks[3], (DEPTH, D_MODEL, IN_COLS), D_MODEL ** -0.5),
        "b_gate": nrm(ks[4], (DEPTH, 2 * D_MODEL), 0.02),
        "hg_lb_logits": nrm(ks[5], (DEPTH + 1, HG_WIDTH), 0.1),
        "hg_norm_g": 1.0 + nrm(ks[6], (DEPTH, HG_HEADS * HG_DV), 0.02),
        "attn_sinks": nrm(ks[7], (DEPTH, AT_QHEADS), 0.5),
        "w_branch_hg": nrm(ks[8], (DEPTH, HG_HEADS * HG_DV, D_MODEL), (HG_HEADS * HG_DV) ** -0.5),
        "w_branch_attn": nrm(ks[9], (DEPTH, AT_WIDTH, D_MODEL), AT_WIDTH ** -0.5),
        "w_out": nrm(ks[10], (DEPTH, D_MODEL, D_MODEL), D_MODEL ** -0.5),
        "ffn_norm_g": 1.0 + nrm(ks[11], (DEPTH, D_MODEL), 0.02),
        "peer_w_q": nrm(ks[12], (DEPTH, D_MODEL, PK_HEADS * D_QUERY), D_MODEL ** -0.5),
        "peer_subkeys": nrm(ks[13], (DEPTH, PK_HEADS, 2, N_KEYS, D_HALF), D_HALF ** -0.5),
        "peer_u": nrm(ks[14], (DEPTH, N_EXPERTS, D_MODEL), D_MODEL ** -0.5),
        "peer_v": nrm(ks[15], (DEPTH, N_EXPERTS, D_MODEL), 0.5),
        "final_norm_g": 1.0 + nrm(ks[16], (D_MODEL,), 0.02),
    }


def reference(x, meta_tokens, mix_norm_g, w_in, b_gate, hg_lb_logits, hg_norm_g, attn_sinks,
              w_branch_hg, w_branch_attn, w_out, ffn_norm_g, peer_w_q, peer_subkeys,
              peer_u, peer_v, final_norm_g):
    B = x.shape[0]
    h = jnp.concatenate(
        [jnp.broadcast_to(meta_tokens[None].astype(x.dtype), (B, N_META, D_MODEL)), x], axis=1)
    L = h.shape[1]
    lbs = jnp.cumsum(jax.nn.softmax(hg_lb_logits.astype(jnp.float32), axis=0), axis=0)
    sizes = (HG_WIDTH, HG_WIDTH, HG_WIDTH, HG_HEADS * HG_DV, AT_WIDTH, AT_KV_WIDTH, AT_KV_WIDTH, D_MODEL, D_MODEL)
    offsets = [int(o) for o in np.cumsum(sizes)[:-1]]
    for l in range(DEPTH):
        y = _rmsnorm(h, mix_norm_g[l])
        z = y @ w_in[l]
        hq, hf, hi, hgate, aq, ak, av, ga, gb = jnp.split(z, offsets, axis=-1)
        o_hg = _hgrn2(hq.reshape(B, L, HG_HEADS, HG_DK), hf.reshape(B, L, HG_HEADS, HG_DK),
                      hi.reshape(B, L, HG_HEADS, HG_DV), lbs[l])
        o_hg = _rmsnorm(o_hg, hg_norm_g[l].reshape(HG_HEADS, HG_DV)).astype(h.dtype)
        o_hg = o_hg.reshape(B, L, HG_HEADS * HG_DV) * jax.nn.silu(hgate)
        o_at = _swa_sinks(aq.reshape(B, L, AT_QHEADS, AT_HDIM), ak.reshape(B, L, AT_KVHEADS, AT_HDIM),
                          av.reshape(B, L, AT_KVHEADS, AT_HDIM), attn_sinks[l])
        gl = (jnp.concatenate([ga, gb], axis=-1) + b_gate[l]).astype(jnp.float32)
        g_hg = jax.nn.sigmoid(gl[..., :D_MODEL]).astype(h.dtype)
        g_at = jax.nn.sigmoid(gl[..., D_MODEL:]).astype(h.dtype)
        m = g_hg * (o_hg @ w_branch_hg[l]) + g_at * (o_at @ w_branch_attn[l])
        h = h + m @ w_out[l]
        h = h + _peer(_rmsnorm(h, ffn_norm_g[l]), peer_w_q[l], peer_subkeys[l], peer_u[l], peer_v[l])
    return _rmsnorm(h, final_norm_g)[:, N_META:]
```

```python
import functools
import math

import jax
import jax.numpy as jnp
from jax import lax
from jax.experimental import pallas as pl
from jax.experimental.pallas import tpu as pltpu

N_META = 16
BLOCK = 128
N_PAD = BLOCK - N_META
HG_HEADS = 4
HG_DK = 128
HG_WIDTH = HG_HEADS * HG_DK
AT_QHEADS = 8
AT_KVHEADS = 2
AT_GROUP = AT_QHEADS // AT_KVHEADS
AT_HDIM = 64
AT_WIDTH = AT_QHEADS * AT_HDIM
AT_KV_WIDTH = AT_KVHEADS * AT_HDIM
PK_HEADS = 8
N_KEYS = 128
TOPK = 16
EPS = 1e-6
MASKED = -1e30
VMEM_LIMIT = 48 * 1024 * 1024

F32 = jnp.float32
BF16 = jnp.bfloat16


def _dot(a, b):
    return jnp.dot(a.astype(BF16), b.astype(BF16), preferred_element_type=F32)


def _dot_nt(a, b):
    return lax.dot_general(a.astype(BF16), b.astype(BF16), (((1,), (1,)), ((), ())),
                           preferred_element_type=F32)


def _rms(x, g):
    return x * lax.rsqrt(jnp.mean(x * x, axis=-1, keepdims=True) + EPS) * g


def _inproj_kernel(h_ref, g_ref, w_ref, zhg_ref, zat_ref, zg_ref):
    yb = _rms(h_ref[...], g_ref[...]).astype(BF16)
    c0 = 4 * HG_WIDTH
    c1 = c0 + AT_WIDTH + 2 * AT_KV_WIDTH
    zhg_ref[...] = jnp.dot(yb, w_ref[:, 0:c0], preferred_element_type=F32)
    zat_ref[...] = jnp.dot(yb, w_ref[:, c0:c1], preferred_element_type=F32)
    zg_ref[...] = jnp.dot(yb, w_ref[:, c1:], preferred_element_type=F32)


def _hgrn_kernel(z_ref, lbl_ref, ng_ref, o_ref, st_ref):
    n = pl.program_id(1)

    @pl.when(n == 0)
    def _():
        st_ref[...] = jnp.zeros_like(st_ref)

    W = HG_WIDTH
    q = z_ref[:, 0:W]
    fl = z_ref[:, W:2 * W]
    v = z_ref[:, 2 * W:3 * W]
    gate = z_ref[:, 3 * W:4 * W]

    l = lbl_ref[...]
    e = jnp.exp(l - jnp.max(l, axis=0, keepdims=True))
    lb = e[0:1] / jnp.sum(e, axis=0, keepdims=True)

    f = lb + (1.0 - lb) * jax.nn.sigmoid(fl)
    row = lax.broadcasted_iota(jnp.int32, (BLOCK, W), 0)
    live = jnp.logical_or(n > 0, row >= N_PAD)
    logf = jnp.where(live, jnp.log(f), 0.0)
    k = jnp.where(live, 1.0 - f, 0.0)

    A = logf
    s = 1
    while s < BLOCK:
        A = A + jnp.where(row >= s, pltpu.roll(A, s, axis=0), 0.0)
        s *= 2

    q_lv = [q]
    k_lv = [k]
    piv_q = jnp.where(row >= 1, pltpu.roll(A, 1, axis=0), 0.0)
    piv_k = A
    m = 1
    while m < BLOCK:
        bit = (row & m) != 0
        q_lv.append(q * jnp.exp(A - piv_q))
        k_lv.append(k * jnp.exp(piv_k - A))
        piv_q = jnp.where(bit, pltpu.roll(piv_q, m, axis=0), piv_q)
        piv_k = jnp.where(bit, piv_k, pltpu.roll(piv_k, BLOCK - m, axis=0))
        m *= 2
    q_in = q * jnp.exp(A)
    k_out = k * jnp.exp(piv_k - A)
    a_end = piv_k[0:1, :]

    ti = lax.broadcasted_iota(jnp.int32, (BLOCK, BLOCK), 0)
    si = lax.broadcasted_iota(jnp.int32, (BLOCK, BLOCK), 1)
    tx = ti ^ si
    ng = ng_ref[...]
    for hh in range(HG_HEADS):
        sl = slice(hh * HG_DK, (hh + 1) * HG_DK)
        scores = jnp.where(ti == si, _dot_nt(q_lv[0][:, sl], k_lv[0][:, sl]), 0.0)
        m = 1
        lv = 1
        while m < BLOCK:
            mask = jnp.logical_and((tx >> (lv - 1)) == 1, (ti & m) != 0)
            scores = scores + jnp.where(mask, _dot_nt(q_lv[lv][:, sl], k_lv[lv][:, sl]), 0.0)
            m *= 2
            lv += 1
        st = st_ref[hh]
        o = _dot(scores, v[:, sl]) + _dot_nt(q_in[:, sl], st)
        st_ref[hh] = st * jnp.exp(a_end[:, sl]) + _dot(v[:, sl].T, k_out[:, sl])
        o = _rms(o, ng[:, sl])
        g = gate[:, sl]
        o_ref[:, sl] = o * (g * jax.nn.sigmoid(g))


def _swa_kernel(sink_ref, cur_ref, prev_ref, first_ref, o_ref):
    n = pl.program_id(1)
    R = AT_GROUP * BLOCK
    NK = 3 * BLOCK
    r = lax.broadcasted_iota(jnp.int32, (R, NK), 0) & (BLOCK - 1)
    c = lax.broadcasted_iota(jnp.int32, (R, NK), 1)
    qpos = n * BLOCK + r
    band = (c > r) & (c <= r + BLOCK) & ((n - 1) * BLOCK + c >= BLOCK) & (c < 2 * BLOCK)
    cm = c - 2 * BLOCK
    meta = (cm >= N_PAD) & (cm <= qpos)
    mask = band | meta
    neg = jnp.finfo(F32).min
    scale = AT_HDIM ** -0.5
    ko = AT_WIDTH
    vo = AT_WIDTH + AT_KV_WIDTH
    for g in range(AT_KVHEADS):
        ks = slice(ko + g * AT_HDIM, ko + (g + 1) * AT_HDIM)
        vs = slice(vo + g * AT_HDIM, vo + (g + 1) * AT_HDIM)
        kall = jnp.concatenate([prev_ref[:, ks], cur_ref[:, ks], first_ref[:, ks]], axis=0)
        vall = jnp.concatenate([prev_ref[:, vs], cur_ref[:, vs], first_ref[:, vs]], axis=0)
        qs = jnp.concatenate(
            [cur_ref[:, (g * AT_GROUP + j) * AT_HDIM:(g * AT_GROUP + j + 1) * AT_HDIM]
             for j in range(AT_GROUP)], axis=0)
        sink = jnp.concatenate(
            [jnp.full((BLOCK, 1), sink_ref[g * AT_GROUP + j], F32) for j in range(AT_GROUP)],
            axis=0)
        sc = jnp.where(mask, _dot_nt(qs, kall) * scale, neg)
        mx = jnp.maximum(jnp.max(sc, axis=-1, keepdims=True), sink)
        p = jnp.exp(sc - mx)
        den = jnp.sum(p, axis=-1, keepdims=True) + jnp.exp(sink - mx)
        o = _dot(p, vall) / den
        for j in range(AT_GROUP):
            hd = g * AT_GROUP + j
            o_ref[:, hd * AT_HDIM:(hd + 1) * AT_HDIM] = o[j * BLOCK:(j + 1) * BLOCK, :]


def _merge_kernel(h_ref, ohg_ref, oat_ref, zg_ref, bg_ref, wbh_ref, wba_ref, wo_ref, fg_ref,
                  wq_ref, sk_ref, h1_ref, xb_ref, st_ref):
    D = h_ref.shape[-1]
    gl = zg_ref[...] + bg_ref[...]
    g_hg = jax.nn.sigmoid(gl[:, :D])
    g_at = jax.nn.sigmoid(gl[:, D:])
    mrg = (g_hg * jnp.dot(ohg_ref[...].astype(BF16), wbh_ref[...], preferred_element_type=F32)
           + g_at * jnp.dot(oat_ref[...].astype(BF16), wba_ref[...], preferred_element_type=F32))
    h1 = h_ref[...] + jnp.dot(mrg.astype(BF16), wo_ref[...], preferred_element_type=F32)
    h1_ref[...] = h1
    xb = _rms(h1, fg_ref[...]).astype(BF16)
    xb_ref[...] = xb
    qp = jnp.dot(xb, wq_ref[...], preferred_element_type=F32).astype(BF16)
    for hc in range(2 * PK_HEADS):
        st_ref[hc] = lax.dot_general(sk_ref[hc], qp[:, hc * N_KEYS:(hc + 1) * N_KEYS],
                                     (((1,), (1,)), ((), ())), preferred_element_type=F32)


def _top_values(s, count):
    vals = []
    for _ in range(count):
        mx = jnp.max(s, axis=0, keepdims=True)
        vals.append(mx)
        s = jnp.where(s == mx, MASKED, s)
    return jnp.concatenate(vals, axis=0)


def _peer_kernel(st_ref, xb_ref, h1_ref, u_ref, vt_ref, fg_ref, out_ref,
                 e0_ref, e1_ref, tau_ref, at_ref, w_ref, acc_ref):
    e = pl.program_id(1)
    n_e = pl.num_programs(1)
    EB = u_ref.shape[0]
    n_ib = EB // N_KEYS

    @pl.when(e == 0)
    def _():
        acc_ref[...] = jnp.zeros_like(acc_ref)
        for h in range(PK_HEADS):
            s0 = st_ref[2 * h]
            s1 = st_ref[2 * h + 1]
            a = _top_values(s0, TOPK)
            b = _top_values(s1, TOPK)
            cand = [a[0:1] + b]
            for ra in range(1, 8):
                cand.append(a[ra:ra + 1] + b[0:8])
            cand.append(a[8:16] + b[0:1])
            tau = _top_values(jnp.concatenate(cand, axis=0), TOPK)[TOPK - 1:TOPK]
            ea = jnp.exp(a - a[0:1])
            eb = jnp.exp(b - b[0:1])
            z = jnp.zeros_like(tau)
            for ra in range(TOPK):
                sel = jnp.where(a[ra:ra + 1] + b >= tau, eb, 0.0)
                z = z + ea[ra:ra + 1] * jnp.sum(sel, axis=0, keepdims=True)
            tau_ref[h:h + 1, :] = tau
            e0_ref[h] = jnp.exp(s0 - a[0:1]) / z
            e1_ref[h] = jnp.exp(s1 - b[0:1])

    at_ref[...] = lax.dot_general(u_ref[...], xb_ref[...], (((1,), (1,)), ((), ())),
                                  preferred_element_type=F32)

    def ib_body(ii, carry):
        i = e * n_ib + ii
        rows = pl.ds(pl.multiple_of(ii * N_KEYS, N_KEYS), N_KEYS)
        gsum = jnp.zeros((N_KEYS, at_ref.shape[1]), F32)
        for h in range(PK_HEADS):
            s0row = st_ref[2 * h, pl.ds(i, 1), :]
            e0row = e0_ref[h, pl.ds(i, 1), :]
            pair = st_ref[2 * h + 1] + s0row
            gsum = gsum + jnp.where(pair >= tau_ref[h:h + 1, :], e1_ref[h], 0.0) * e0row
        a = at_ref[rows, :]
        gelu = 0.5 * a * (1.0 + lax.erf(a * math.sqrt(0.5)))
        w_ref[rows, :] = (gelu * gsum).astype(BF16)
        return carry

    lax.fori_loop(0, n_ib, ib_body, 0)
    acc_ref[...] += jnp.dot(vt_ref[...], w_ref[...], preferred_element_type=F32)

    @pl.when(e == n_e - 1)
    def _():
        out_ref[...] = _rms(h1_ref[...] + acc_ref[...].T, fg_ref[...])


def _tile(total, pref):
    t = pref
    while total % t:
        t //= 2
    return t


def _const_spec(shape):
    return pl.BlockSpec(shape, lambda *_: (0,) * len(shape))


def kernel(x, meta_tokens, mix_norm_g, w_in, b_gate, hg_lb_logits, hg_norm_g, attn_sinks,
           w_branch_hg, w_branch_attn, w_out, ffn_norm_g, peer_w_q, peer_subkeys,
           peer_u, peer_v, final_norm_g):
    B, S, D = x.shape
    assert S % BLOCK == 0 and w_in.shape[0] == 1
    P = S + BLOCK
    NB = P // BLOCK
    T = B * P
    n_exp = peer_u.shape[1]

    h0 = jnp.concatenate(
        [jnp.zeros((B, N_PAD, D), x.dtype),
         jnp.broadcast_to(meta_tokens[None].astype(x.dtype), (B, N_META, D)), x],
        axis=1).reshape(T, D)

    w_in_b = w_in[0].astype(BF16)
    in_cols = w_in_b.shape[1]
    c_hg = 4 * HG_WIDTH
    c_at = AT_WIDTH + 2 * AT_KV_WIDTH
    c_g = 2 * D

    TM = _tile(T, 256)
    z_hg, z_at, z_g = pl.pallas_call(
        _inproj_kernel,
        grid=(T // TM,),
        in_specs=[pl.BlockSpec((TM, D), lambda i: (i, 0)),
                  _const_spec((1, D)),
                  _const_spec((D, in_cols))],
        out_specs=[pl.BlockSpec((TM, c_hg), lambda i: (i, 0)),
                   pl.BlockSpec((TM, c_at), lambda i: (i, 0)),
                   pl.BlockSpec((TM, c_g), lambda i: (i, 0))],
        out_shape=[jax.ShapeDtypeStruct((T, c_hg), F32),
                   jax.ShapeDtypeStruct((T, c_at), F32),
                   jax.ShapeDtypeStruct((T, c_g), F32)],
        compiler_params=pltpu.CompilerParams(dimension_semantics=("parallel",),
                                             vmem_limit_bytes=VMEM_LIMIT),
        name="inproj",
    )(h0, mix_norm_g[0:1], w_in_b)

    o_hg = pl.pallas_call(
        _hgrn_kernel,
        grid=(B, NB),
        in_specs=[pl.BlockSpec((BLOCK, c_hg), lambda b, n: (b * NB + n, 0)),
                  _const_spec((2, HG_WIDTH)),
                  _const_spec((1, HG_WIDTH))],
        out_specs=pl.BlockSpec((BLOCK, HG_WIDTH), lambda b, n: (b * NB + n, 0)),
        out_shape=jax.ShapeDtypeStruct((T, HG_WIDTH), F32),
        scratch_shapes=[pltpu.VMEM((HG_HEADS, HG_DK, HG_DK), F32)],
        compiler_params=pltpu.CompilerParams(dimension_semantics=("parallel", "arbitrary"),
                                             vmem_limit_bytes=VMEM_LIMIT),
        name="hgrn",
    )(z_hg, hg_lb_logits.astype(F32), hg_norm_g[0:1])

    o_at = pl.pallas_call(
        _swa_kernel,
        grid=(B, NB),
        in_specs=[pl.BlockSpec(memory_space=pltpu.SMEM),
                  pl.BlockSpec((BLOCK, c_at), lambda b, n: (b * NB + n, 0)),
                  pl.BlockSpec((BLOCK, c_at), lambda b, n: (b * NB + jnp.maximum(n - 1, 0), 0)),
                  pl.BlockSpec((BLOCK, c_at), lambda b, n: (b * NB, 0))],
        out_specs=pl.BlockSpec((BLOCK, AT_WIDTH), lambda b, n: (b * NB + n, 0)),
        out_shape=jax.ShapeDtypeStruct((T, AT_WIDTH), F32),
        compiler_params=pltpu.CompilerParams(dimension_semantics=("parallel", "arbitrary"),
                                             vmem_limit_bytes=VMEM_LIMIT),
        name="swa",
    )(attn_sinks[0].astype(F32), z_at, z_at, z_at)

    n_hc = 2 * PK_HEADS
    sk = peer_subkeys[0].reshape(n_hc, N_KEYS, -1).astype(BF16)
    d_half = sk.shape[-1]
    TM4 = _tile(T, 256)
    h1, xb, st = pl.pallas_call(
        _merge_kernel,
        grid=(T // TM4,),
        in_specs=[pl.BlockSpec((TM4, D), lambda i: (i, 0)),
                  pl.BlockSpec((TM4, HG_WIDTH), lambda i: (i, 0)),
                  pl.BlockSpec((TM4, AT_WIDTH), lambda i: (i, 0)),
                  pl.BlockSpec((TM4, c_g), lambda i: (i, 0)),
                  _const_spec((1, c_g)),
                  _const_spec((HG_WIDTH, D)),
                  _const_spec((AT_WIDTH, D)),
                  _const_spec((D, D)),
                  _const_spec((1, D)),
                  _const_spec((D, n_hc * d_half)),
                  _const_spec((n_hc, N_KEYS, d_half))],
        out_specs=[pl.BlockSpec((TM4, D), lambda i: (i, 0)),
                   pl.BlockSpec((TM4, D), lambda i: (i, 0)),
                   pl.BlockSpec((n_hc, N_KEYS, TM4), lambda i: (0, 0, i))],
        out_shape=[jax.ShapeDtypeStruct((T, D), F32),
                   jax.ShapeDtypeStruct((T, D), BF16),
                   jax.ShapeDtypeStruct((n_hc, N_KEYS, T), F32)],
        compiler_params=pltpu.CompilerParams(dimension_semantics=("parallel",),
                                             vmem_limit_bytes=VMEM_LIMIT),
        name="merge",
    )(h0, o_hg, o_at, z_g, b_gate[0:1], w_branch_hg[0].astype(BF16),
      w_branch_attn[0].astype(BF16), w_out[0].astype(BF16), ffn_norm_g[0:1],
      peer_w_q[0].astype(BF16), sk)

    TB = _tile(T, 256)
    EB = 1024
    u_b = peer_u[0].astype(BF16)
    vt_b = peer_v[0].astype(BF16).T
    out = pl.pallas_call(
        _peer_kernel,
        grid=(T // TB, n_exp // EB),
        in_specs=[pl.BlockSpec((n_hc, N_KEYS, TB), lambda i, e: (0, 0, i)),
                  pl.BlockSpec((TB, D), lambda i, e: (i, 0)),
                  pl.BlockSpec((TB, D), lambda i, e: (i, 0)),
                  pl.BlockSpec((EB, D), lambda i, e: (e, 0)),
                  pl.BlockSpec((D, EB), lambda i, e: (0, e)),
                  _const_spec((1, D))],
        out_specs=pl.BlockSpec((TB, D), lambda i, e: (i, 0)),
        out_shape=jax.ShapeDtypeStruct((T, D), F32),
        scratch_shapes=[pltpu.VMEM((PK_HEADS, N_KEYS, TB), F32),
                        pltpu.VMEM((PK_HEADS, N_KEYS, TB), F32),
                        pltpu.VMEM((PK_HEADS, TB), F32),
                        pltpu.VMEM((EB, TB), F32),
                        pltpu.VMEM((EB, TB), BF16),
                        pltpu.VMEM((D, TB), F32)],
        compiler_params=pltpu.CompilerParams(dimension_semantics=("parallel", "arbitrary"),
                                             vmem_limit_bytes=VMEM_LIMIT),
        name="peer",
    )(st, xb, h1, u_b, vt_b, final_norm_g.reshape(1, D))

    return out.reshape(B, P, D)[:, BLOCK:]
```

```python
import functools
import math

import jax
import jax.numpy as jnp
from jax import lax
from jax.experimental import pallas as pl
from jax.experimental.pallas import tpu as pltpu

N_META = 16
BLOCK = 128
N_PAD = BLOCK - N_META
HG_HEADS = 4
HG_DK = 128
HG_WIDTH = HG_HEADS * HG_DK
AT_QHEADS = 8
AT_KVHEADS = 2
AT_GROUP = AT_QHEADS // AT_KVHEADS
AT_HDIM = 64
AT_WIDTH = AT_QHEADS * AT_HDIM
AT_KV_WIDTH = AT_KVHEADS * AT_HDIM
PK_HEADS = 8
N_KEYS = 128
TOPK = 16
EPS = 1e-6
MASKED = -1e30
VMEM_LIMIT = 48 * 1024 * 1024

F32 = jnp.float32
BF16 = jnp.bfloat16


def _dot(a, b):
    return jnp.dot(a.astype(BF16), b.astype(BF16), preferred_element_type=F32)


def _dot_nt(a, b):
    return lax.dot_general(a.astype(BF16), b.astype(BF16), (((1,), (1,)), ((), ())),
                           preferred_element_type=F32)


def _rms(x, g):
    return x * lax.rsqrt(jnp.mean(x * x, axis=-1, keepdims=True) + EPS) * g


def _inproj_kernel(h_ref, g_ref, w_ref, zhg_ref, zat_ref, zg_ref):
    yb = _rms(h_ref[...], g_ref[...]).astype(BF16)
    c0 = 4 * HG_WIDTH
    c1 = c0 + AT_WIDTH + 2 * AT_KV_WIDTH
    zhg_ref[...] = jnp.dot(yb, w_ref[:, 0:c0], preferred_element_type=F32)
    zat_ref[...] = jnp.dot(yb, w_ref[:, c0:c1], preferred_element_type=F32)
    zg_ref[...] = jnp.dot(yb, w_ref[:, c1:], preferred_element_type=F32)


def _hgrn_kernel(z_ref, lbl_ref, ng_ref, o_ref, st_ref):
    n = pl.program_id(1)

    @pl.when(n == 0)
    def _():
        st_ref[...] = jnp.zeros_like(st_ref)

    W = HG_WIDTH
    q = z_ref[:, 0:W]
    fl = z_ref[:, W:2 * W]
    v = z_ref[:, 2 * W:3 * W]
    gate = z_ref[:, 3 * W:4 * W]

    l = lbl_ref[...]
    e = jnp.exp(l - jnp.max(l, axis=0, keepdims=True))
    lb = e[0:1] / jnp.sum(e, axis=0, keepdims=True)

    f = lb + (1.0 - lb) * jax.nn.sigmoid(fl)
    row = lax.broadcasted_iota(jnp.int32, (BLOCK, W), 0)
    live = jnp.logical_or(n > 0, row >= N_PAD)
    logf = jnp.where(live, jnp.log(f), 0.0)
    k = jnp.where(live, 1.0 - f, 0.0)

    A = logf
    s = 1
    while s < BLOCK:
        A = A + jnp.where(row >= s, pltpu.roll(A, s, axis=0), 0.0)
        s *= 2

    q_lv = [q]
    k_lv = [k]
    piv_q = jnp.where(row >= 1, pltpu.roll(A, 1, axis=0), 0.0)
    piv_k = A
    m = 1
    while m < BLOCK:
        bit = (row & m) != 0
        q_lv.append(q * jnp.exp(A - piv_q))
        k_lv.append(k * jnp.exp(piv_k - A))
        piv_q = jnp.where(bit, pltpu.roll(piv_q, m, axis=0), piv_q)
        piv_k = jnp.where(bit, piv_k, pltpu.roll(piv_k, BLOCK - m, axis=0))
        m *= 2
    q_in = q * jnp.exp(A)
    k_out = k * jnp.exp(piv_k - A)
    a_end = piv_k[0:1, :]

    ti = lax.broadcasted_iota(jnp.int32, (BLOCK, BLOCK), 0)
    si = lax.broadcasted_iota(jnp.int32, (BLOCK, BLOCK), 1)
    tx = ti ^ si
    ng = ng_ref[...]
    for hh in range(HG_HEADS):
        sl = slice(hh * HG_DK, (hh + 1) * HG_DK)
        scores = jnp.where(ti == si, _dot_nt(q_lv[0][:, sl], k_lv[0][:, sl]), 0.0)
        m = 1
        lv = 1
        while m < BLOCK:
            mask = jnp.logical_and((tx >> (lv - 1)) == 1, (ti & m) != 0)
            scores = scores + jnp.where(mask, _dot_nt(q_lv[lv][:, sl], k_lv[lv][:, sl]), 0.0)
            m *= 2
            lv += 1
        st = st_ref[hh]
        o = _dot(scores, v[:, sl]) + _dot_nt(q_in[:, sl], st)
        st_ref[hh] = st * jnp.exp(a_end[:, sl]) + _dot(v[:, sl].T, k_out[:, sl])
        o = _rms(o, ng[:, sl])
        g = gate[:, sl]
        o_ref[:, sl] = o * (g * jax.nn.sigmoid(g))


def _swa_kernel(sink_ref, cur_ref, prev_ref, first_ref, o_ref):
    n = pl.program_id(1)
    R = AT_GROUP * BLOCK
    NK = 3 * BLOCK
    r = lax.broadcasted_iota(jnp.int32, (R, NK), 0) & (BLOCK - 1)
    c = lax.broadcasted_iota(jnp.int32, (R, NK), 1)
    qpos = n * BLOCK + r
    band = (c > r) & (c <= r + BLOCK) & ((n - 1) * BLOCK + c >= BLOCK) & (c < 2 * BLOCK)
    cm = c - 2 * BLOCK
    meta = (cm >= N_PAD) & (cm <= qpos)
    mask = band | meta
    neg = jnp.finfo(F32).min
    scale = AT_HDIM ** -0.5
    ko = AT_WIDTH
    vo = AT_WIDTH + AT_KV_WIDTH
    for g in range(AT_KVHEADS):
        ks = slice(ko + g * AT_HDIM, ko + (g + 1) * AT_HDIM)
        vs = slice(vo + g * AT_HDIM, vo + (g + 1) * AT_HDIM)
        kall = jnp.concatenate([prev_ref[:, ks], cur_ref[:, ks], first_ref[:, ks]], axis=0)
        vall = jnp.concatenate([prev_ref[:, vs], cur_ref[:, vs], first_ref[:, vs]], axis=0)
        qs = jnp.concatenate(
            [cur_ref[:, (g * AT_GROUP + j) * AT_HDIM:(g * AT_GROUP + j + 1) * AT_HDIM]
             for j in range(AT_GROUP)], axis=0)
        sink = jnp.concatenate(
            [jnp.full((BLOCK, 1), sink_ref[g * AT_GROUP + j], F32) for j in range(AT_GROUP)],
            axis=0)
        sc = jnp.where(mask, _dot_nt(qs, kall) * scale, neg)
        mx = jnp.maximum(jnp.max(sc, axis=-1, keepdims=True), sink)
        p = jnp.exp(sc - mx)
        den = jnp.sum(p, axis=-1, keepdims=True) + jnp.exp(sink - mx)
        o = _dot(p, vall) / den
        for j in range(AT_GROUP):
            hd = g * AT_GROUP + j
            o_ref[:, hd * AT_HDIM:(hd + 1) * AT_HDIM] = o[j * BLOCK:(j + 1) * BLOCK, :]


def _merge_kernel(h_ref, ohg_ref, oat_ref, zg_ref, bg_ref, wbh_ref, wba_ref, wo_ref, fg_ref,
                  wq_ref, sk_ref, h1_ref, xb_ref, st_ref):
    D = h_ref.shape[-1]
    gl = zg_ref[...] + bg_ref[...]
    g_hg = jax.nn.sigmoid(gl[:, :D])
    g_at = jax.nn.sigmoid(gl[:, D:])
    mrg = (g_hg * jnp.dot(ohg_ref[...].astype(BF16), wbh_ref[...], preferred_element_type=F32)
           + g_at * jnp.dot(oat_ref[...].astype(BF16), wba_ref[...], preferred_element_type=F32))
    h1 = h_ref[...] + jnp.dot(mrg.astype(BF16), wo_ref[...], preferred_element_type=F32)
    h1_ref[...] = h1
    xb = _rms(h1, fg_ref[...]).astype(BF16)
    xb_ref[...] = xb
    qp = jnp.dot(xb, wq_ref[...], preferred_element_type=F32).astype(BF16)
    for hc in range(2 * PK_HEADS):
        st_ref[hc] = lax.dot_general(sk_ref[hc], qp[:, hc * N_KEYS:(hc + 1) * N_KEYS],
                                     (((1,), (1,)), ((), ())), preferred_element_type=F32)


def _top_values(s, count):
    vals = []
    for _ in range(count):
        mx = jnp.max(s, axis=0, keepdims=True)
        vals.append(mx)
        s = jnp.where(s == mx, MASKED, s)
    return jnp.concatenate(vals, axis=0)


def _peer_kernel(st_ref, xb_ref, h1_ref, u_ref, vt_ref, fg_ref, out_ref,
                 e0_ref, e1_ref, tau_ref, at_ref, w_ref, acc_ref):
    e = pl.program_id(1)
    n_e = pl.num_programs(1)
    EB = u_ref.shape[0]
    n_ib = EB // N_KEYS

    @pl.when(e == 0)
    def _():
        acc_ref[...] = jnp.zeros_like(acc_ref)
        for h in range(PK_HEADS):
            s0 = st_ref[2 * h]
            s1 = st_ref[2 * h + 1]
            a = _top_values(s0, TOPK)
            b = _top_values(s1, TOPK)
            cand = [a[0:1] + b]
            for ra in range(1, 8):
                cand.append(a[ra:ra + 1] + b[0:8])
            cand.append(a[8:16] + b[0:1])
            tau = _top_values(jnp.concatenate(cand, axis=0), TOPK)[TOPK - 1:TOPK]
            ea = jnp.exp(a - a[0:1])
            eb = jnp.exp(b - b[0:1])
            z = jnp.zeros_like(tau)
            for ra in range(TOPK):
                sel = jnp.where(a[ra:ra + 1] + b >= tau, eb, 0.0)
                z = z + ea[ra:ra + 1] * jnp.sum(sel, axis=0, keepdims=True)
            tau_ref[h:h + 1, :] = tau
            e0_ref[h] = jnp.exp(s0 - a[0:1]) / z
            e1_ref[h] = jnp.exp(s1 - b[0:1])

    at_ref[...] = lax.dot_general(u_ref[...], xb_ref[...], (((1,), (1,)), ((), ())),
                                  preferred_element_type=F32)

    def ib_body(ii, carry):
        i = e * n_ib + ii
        rows = pl.ds(pl.multiple_of(ii * N_KEYS, N_KEYS), N_KEYS)
        gsum = jnp.zeros((N_KEYS, at_ref.shape[1]), F32)
        for h in range(PK_HEADS):
            s0row = st_ref[2 * h, pl.ds(i, 1), :]
            e0row = e0_ref[h, pl.ds(i, 1), :]
            pair = st_ref[2 * h + 1] + s0row
            gsum = gsum + jnp.where(pair >= tau_ref[h:h + 1, :], e1_ref[h], 0.0) * e0row
        a = at_ref[rows, :]
        gelu = 0.5 * a * (1.0 + lax.erf(a * math.sqrt(0.5)))
        w_ref[rows, :] = (gelu * gsum).astype(BF16)
        return carry

    lax.fori_loop(0, n_ib, ib_body, 0)
    acc_ref[...] += jnp.dot(vt_ref[...], w_ref[...], preferred_element_type=F32)

    @pl.when(e == n_e - 1)
    def _():
        out_ref[...] = _rms(h1_ref[...] + acc_ref[...].T, fg_ref[...])


def _tile(total, pref):
    t = pref
    while total % t:
        t //= 2
    return t


def _const_spec(shape):
    return pl.BlockSpec(shape, lambda *_: (0,) * len(shape))


def kernel(x, meta_tokens, mix_norm_g, w_in, b_gate, hg_lb_logits, hg_norm_g, attn_sinks,
           w_branch_hg, w_branch_attn, w_out, ffn_norm_g, peer_w_q, peer_subkeys,
           peer_u, peer_v, final_norm_g):
    B, S, D = x.shape
    assert S % BLOCK == 0 and w_in.shape[0] == 1
    P = S + BLOCK
    NB = P // BLOCK
    T = B * P
    n_exp = peer_u.shape[1]

    h0 = jnp.concatenate(
        [jnp.zeros((B, N_PAD, D), x.dtype),
         jnp.broadcast_to(meta_tokens[None].astype(x.dtype), (B, N_META, D)), x],
        axis=1).reshape(T, D)

    w_in_b = w_in[0].astype(BF16)
    in_cols = w_in_b.shape[1]
    c_hg = 4 * HG_WIDTH
    c_at = AT_WIDTH + 2 * AT_KV_WIDTH
    c_g = 2 * D

    TM = _tile(T, 256)
    z_hg, z_at, z_g = pl.pallas_call(
        _inproj_kernel,
        grid=(T // TM,),
        in_specs=[pl.BlockSpec((TM, D), lambda i: (i, 0)),
                  _const_spec((1, D)),
                  _const_spec((D, in_cols))],
        out_specs=[pl.BlockSpec((TM, c_hg), lambda i: (i, 0)),
                   pl.BlockSpec((TM, c_at), lambda i: (i, 0)),
                   pl.BlockSpec((TM, c_g), lambda i: (i, 0))],
        out_shape=[jax.ShapeDtypeStruct((T, c_hg), F32),
                   jax.ShapeDtypeStruct((T, c_at), F32),
                   jax.ShapeDtypeStruct((T, c_g), F32)],
        compiler_params=pltpu.CompilerParams(dimension_semantics=("parallel",),
                                             vmem_limit_bytes=VMEM_LIMIT),
        name="inproj",
    )(h0, mix_norm_g[0:1], w_in_b)

    o_hg = pl.pallas_call(
        _hgrn_kernel,
        grid=(B, NB),
        in_specs=[pl.BlockSpec((BLOCK, c_hg), lambda b, n: (b * NB + n, 0)),
                  _const_spec((2, HG_WIDTH)),
                  _const_spec((1, HG_WIDTH))],
        out_specs=pl.BlockSpec((BLOCK, HG_WIDTH), lambda b, n: (b * NB + n, 0)),
        out_shape=jax.ShapeDtypeStruct((T, HG_WIDTH), F32),
        scratch_shapes=[pltpu.VMEM((HG_HEADS, HG_DK, HG_DK), F32)],
        compiler_params=pltpu.CompilerParams(dimension_semantics=("parallel", "arbitrary"),
                                             vmem_limit_bytes=VMEM_LIMIT),
        name="hgrn",
    )(z_hg, hg_lb_logits.astype(F32), hg_norm_g[0:1])

    o_at = pl.pallas_call(
        _swa_kernel,
        grid=(B, NB),
        in_specs=[pl.BlockSpec(memory_space=pltpu.SMEM),
                  pl.BlockSpec((BLOCK, c_at), lambda b, n: (b * NB + n, 0)),
                  pl.BlockSpec((BLOCK, c_at), lambda b, n: (b * NB + jnp.maximum(n - 1, 0), 0)),
                  pl.BlockSpec((BLOCK, c_at), lambda b, n: (b * NB, 0))],
        out_specs=pl.BlockSpec((BLOCK, AT_WIDTH), lambda b, n: (b * NB + n, 0)),
        out_shape=jax.ShapeDtypeStruct((T, AT_WIDTH), F32),
        compiler_params=pltpu.CompilerParams(dimension_semantics=("parallel", "arbitrary"),
                                             vmem_limit_bytes=VMEM_LIMIT),
        name="swa",
    )(attn_sinks[0].astype(F32), z_at, z_at, z_at)

    n_hc = 2 * PK_HEADS
    sk = peer_subkeys[0].reshape(n_hc, N_KEYS, -1).astype(BF16)
    d_half = sk.shape[-1]
    TM4 = _tile(T, 256)
    h1, xb, st = pl.pallas_call(
        _merge_kernel,
        grid=(T // TM4,),
        in_specs=[pl.BlockSpec((TM4, D), lambda i: (i, 0)),
                  pl.BlockSpec((TM4, HG_WIDTH), lambda i: (i, 0)),
                  pl.BlockSpec((TM4, AT_WIDTH), lambda i: (i, 0)),
                  pl.BlockSpec((TM4, c_g), lambda i: (i, 0)),
                  _const_spec((1, c_g)),
                  _const_spec((HG_WIDTH, D)),
                  _const_spec((AT_WIDTH, D)),
                  _const_spec((D, D)),
                  _const_spec((1, D)),
                  _const_spec((D, n_hc * d_half)),
                  _const_spec((n_hc, N_KEYS, d_half))],
        out_specs=[pl.BlockSpec((TM4, D), lambda i: (i, 0)),
                   pl.BlockSpec((TM4, D), lambda i: (i, 0)),
                   pl.BlockSpec((n_hc, N_KEYS, TM4), lambda i: (0, 0, i))],
        out_shape=[jax.ShapeDtypeStruct((T, D), F32),
                   jax.ShapeDtypeStruct((T, D), BF16),
                   jax.ShapeDtypeStruct((n_hc, N_KEYS, T), F32)],
        compiler_params=pltpu.CompilerParams(dimension_semantics=("parallel",),
                                             vmem_limit_bytes=VMEM_LIMIT),
        name="merge",
    )(h0, o_hg, o_at, z_g, b_gate[0:1], w_branch_hg[0].astype(BF16),
      w_branch_attn[0].astype(BF16), w_out[0].astype(BF16), ffn_norm_g[0:1],
      peer_w_q[0].astype(BF16), sk)

    TB = _tile(T, 512)
    EB = 1024
    u_b = peer_u[0].astype(BF16)
    vt_b = peer_v[0].astype(BF16).T
    out = pl.pallas_call(
        _peer_kernel,
        grid=(T // TB, n_exp // EB),
        in_specs=[pl.BlockSpec((n_hc, N_KEYS, TB), lambda i, e: (0, 0, i)),
                  pl.BlockSpec((TB, D), lambda i, e: (i, 0)),
                  pl.BlockSpec((TB, D), lambda i, e: (i, 0)),
                  pl.BlockSpec((EB, D), lambda i, e: (e, 0)),
                  pl.BlockSpec((D, EB), lambda i, e: (0, e)),
                  _const_spec((1, D))],
        out_specs=pl.BlockSpec((TB, D), lambda i, e: (i, 0)),
        out_shape=jax.ShapeDtypeStruct((T, D), F32),
        scratch_shapes=[pltpu.VMEM((PK_HEADS, N_KEYS, TB), F32),
                        pltpu.VMEM((PK_HEADS, N_KEYS, TB), F32),
                        pltpu.VMEM((PK_HEADS, TB), F32),
                        pltpu.VMEM((EB, TB), F32),
                        pltpu.VMEM((EB, TB), BF16),
                        pltpu.VMEM((D, TB), F32)],
        compiler_params=pltpu.CompilerParams(dimension_semantics=("parallel", "arbitrary"),
                                             vmem_limit_bytes=VMEM_LIMIT),
        name="peer",
    )(st, xb, h1, u_b, vt_b, final_norm_g.reshape(1, D))

    return out.reshape(B, P, D)[:, BLOCK:]
```

```python
import functools
import math

import jax
import jax.numpy as jnp
from jax import lax
from jax.experimental import pallas as pl
from jax.experimental.pallas import tpu as pltpu

N_META = 16
BLOCK = 128
N_PAD = BLOCK - N_META
HG_HEADS = 4
HG_DK = 128
HG_WIDTH = HG_HEADS * HG_DK
AT_QHEADS = 8
AT_KVHEADS = 2
AT_GROUP = AT_QHEADS // AT_KVHEADS
AT_HDIM = 64
AT_WIDTH = AT_QHEADS * AT_HDIM
AT_KV_WIDTH = AT_KVHEADS * AT_HDIM
PK_HEADS = 8
N_KEYS = 128
TOPK = 16
EPS = 1e-6
MASKED = -1e30
VMEM_LIMIT = 48 * 1024 * 1024

F32 = jnp.float32
BF16 = jnp.bfloat16


def _dot(a, b):
    return jnp.dot(a.astype(BF16), b.astype(BF16), preferred_element_type=F32)


def _dot_nt(a, b):
    return lax.dot_general(a.astype(BF16), b.astype(BF16), (((1,), (1,)), ((), ())),
                           preferred_element_type=F32)


def _rms(x, g):
    return x * lax.rsqrt(jnp.mean(x * x, axis=-1, keepdims=True) + EPS) * g


def _inproj_kernel(h_ref, g_ref, w_ref, zhg_ref, zat_ref, zg_ref):
    yb = _rms(h_ref[...], g_ref[...]).astype(BF16)
    c0 = 4 * HG_WIDTH
    c1 = c0 + AT_WIDTH + 2 * AT_KV_WIDTH
    zhg_ref[...] = jnp.dot(yb, w_ref[:, 0:c0], preferred_element_type=F32)
    zat_ref[...] = jnp.dot(yb, w_ref[:, c0:c1], preferred_element_type=F32)
    zg_ref[...] = jnp.dot(yb, w_ref[:, c1:], preferred_element_type=F32)


def _hgrn_kernel(z_ref, lbl_ref, ng_ref, o_ref, st_ref):
    n = pl.program_id(1)

    @pl.when(n == 0)
    def _():
        st_ref[...] = jnp.zeros_like(st_ref)

    W = HG_WIDTH
    q = z_ref[:, 0:W]
    fl = z_ref[:, W:2 * W]
    v = z_ref[:, 2 * W:3 * W]
    gate = z_ref[:, 3 * W:4 * W]

    l = lbl_ref[...]
    e = jnp.exp(l - jnp.max(l, axis=0, keepdims=True))
    lb = e[0:1] / jnp.sum(e, axis=0, keepdims=True)

    f = lb + (1.0 - lb) * jax.nn.sigmoid(fl)
    row = lax.broadcasted_iota(jnp.int32, (BLOCK, W), 0)
    live = jnp.logical_or(n > 0, row >= N_PAD)
    logf = jnp.where(live, jnp.log(f), 0.0)
    k = jnp.where(live, 1.0 - f, 0.0)

    A = logf
    s = 1
    while s < BLOCK:
        A = A + jnp.where(row >= s, pltpu.roll(A, s, axis=0), 0.0)
        s *= 2

    q_lv = [q]
    k_lv = [k]
    piv_q = jnp.where(row >= 1, pltpu.roll(A, 1, axis=0), 0.0)
    piv_k = A
    m = 1
    while m < BLOCK:
        bit = (row & m) != 0
        q_lv.append(q * jnp.exp(A - piv_q))
        k_lv.append(k * jnp.exp(piv_k - A))
        piv_q = jnp.where(bit, pltpu.roll(piv_q, m, axis=0), piv_q)
        piv_k = jnp.where(bit, piv_k, pltpu.roll(piv_k, BLOCK - m, axis=0))
        m *= 2
    q_in = q * jnp.exp(A)
    k_out = k * jnp.exp(piv_k - A)
    a_end = piv_k[0:1, :]

    ti = lax.broadcasted_iota(jnp.int32, (BLOCK, BLOCK), 0)
    si = lax.broadcasted_iota(jnp.int32, (BLOCK, BLOCK), 1)
    tx = ti ^ si
    ng = ng_ref[...]
    for hh in range(HG_HEADS):
        sl = slice(hh * HG_DK, (hh + 1) * HG_DK)
        scores = jnp.where(ti == si, _dot_nt(q_lv[0][:, sl], k_lv[0][:, sl]), 0.0)
        m = 1
        lv = 1
        while m < BLOCK:
            mask = jnp.logical_and((tx >> (lv - 1)) == 1, (ti & m) != 0)
            scores = scores + jnp.where(mask, _dot_nt(q_lv[lv][:, sl], k_lv[lv][:, sl]), 0.0)
            m *= 2
            lv += 1
        st = st_ref[hh]
        o = _dot(scores, v[:, sl]) + _dot_nt(q_in[:, sl], st)
        st_ref[hh] = st * jnp.exp(a_end[:, sl]) + _dot(v[:, sl].T, k_out[:, sl])
        o = _rms(o, ng[:, sl])
        g = gate[:, sl]
        o_ref[:, sl] = o * (g * jax.nn.sigmoid(g))


def _swa_kernel(sink_ref, cur_ref, prev_ref, first_ref, o_ref):
    n = pl.program_id(1)
    R = AT_GROUP * BLOCK
    NK = 3 * BLOCK
    r = lax.broadcasted_iota(jnp.int32, (R, NK), 0) & (BLOCK - 1)
    c = lax.broadcasted_iota(jnp.int32, (R, NK), 1)
    qpos = n * BLOCK + r
    band = (c > r) & (c <= r + BLOCK) & ((n - 1) * BLOCK + c >= BLOCK) & (c < 2 * BLOCK)
    cm = c - 2 * BLOCK
    meta = (cm >= N_PAD) & (cm <= qpos)
    mask = band | meta
    neg = jnp.finfo(F32).min
    scale = AT_HDIM ** -0.5
    ko = AT_WIDTH
    vo = AT_WIDTH + AT_KV_WIDTH
    for bi in range(cur_ref.shape[0]):
        for g in range(AT_KVHEADS):
            ks = slice(ko + g * AT_HDIM, ko + (g + 1) * AT_HDIM)
            vs = slice(vo + g * AT_HDIM, vo + (g + 1) * AT_HDIM)
            kall = jnp.concatenate([prev_ref[bi, :, ks], cur_ref[bi, :, ks], first_ref[bi, :, ks]],
                                   axis=0)
            vall = jnp.concatenate([prev_ref[bi, :, vs], cur_ref[bi, :, vs], first_ref[bi, :, vs]],
                                   axis=0)
            qs = jnp.concatenate(
                [cur_ref[bi, :, (g * AT_GROUP + j) * AT_HDIM:(g * AT_GROUP + j + 1) * AT_HDIM]
                 for j in range(AT_GROUP)], axis=0)
            sink = jnp.concatenate(
                [jnp.full((BLOCK, 1), sink_ref[g * AT_GROUP + j], F32) for j in range(AT_GROUP)],
                axis=0)
            sc = jnp.where(mask, _dot_nt(qs, kall) * scale, neg)
            mx = jnp.maximum(jnp.max(sc, axis=-1, keepdims=True), sink)
            p = jnp.exp(sc - mx)
            den = jnp.sum(p, axis=-1, keepdims=True) + jnp.exp(sink - mx)
            o = _dot(p, vall) / den
            for j in range(AT_GROUP):
                hd = g * AT_GROUP + j
                o_ref[bi, :, hd * AT_HDIM:(hd + 1) * AT_HDIM] = o[j * BLOCK:(j + 1) * BLOCK, :]


def _merge_kernel(h_ref, ohg_ref, oat_ref, zg_ref, bg_ref, wbh_ref, wba_ref, wo_ref, fg_ref,
                  wq_ref, sk_ref, h1_ref, xb_ref, st_ref):
    D = h_ref.shape[-1]
    gl = zg_ref[...] + bg_ref[...]
    g_hg = jax.nn.sigmoid(gl[:, :D])
    g_at = jax.nn.sigmoid(gl[:, D:])
    mrg = (g_hg * jnp.dot(ohg_ref[...].astype(BF16), wbh_ref[...], preferred_element_type=F32)
           + g_at * jnp.dot(oat_ref[...].astype(BF16), wba_ref[...], preferred_element_type=F32))
    h1 = h_ref[...] + jnp.dot(mrg.astype(BF16), wo_ref[...], preferred_element_type=F32)
    h1_ref[...] = h1
    xn = _rms(h1, fg_ref[...])
    xb = xn.astype(BF16)
    xb_ref[...] = xn.T.astype(BF16)
    qp = jnp.dot(xb, wq_ref[...], preferred_element_type=F32).astype(BF16)
    for hc in range(2 * PK_HEADS):
        st_ref[hc] = lax.dot_general(sk_ref[hc], qp[:, hc * N_KEYS:(hc + 1) * N_KEYS],
                                     (((1,), (1,)), ((), ())), preferred_element_type=F32)


SUBLANES = 8


def _oddeven_mergesort_pairs(n):
    pairs = []
    p = 1
    while p < n:
        k = p
        while k >= 1:
            for j in range(k % p, n - k, 2 * k):
                for i in range(min(k, n - j - k)):
                    if (i + j) // (2 * p) == (i + j + k) // (2 * p):
                        pairs.append((i + j, i + j + k))
            k //= 2
        p *= 2
    return pairs


def _exchange(x, i, j):
    x[i], x[j] = jnp.maximum(x[i], x[j]), jnp.minimum(x[i], x[j])


def _top16_sorted(slabs):
    x = list(slabs)
    for i, j in _oddeven_mergesort_pairs(TOPK):
        _exchange(x, i, j)
    shift = SUBLANES // 2
    while shift >= 1:
        y = [pltpu.roll(v, shift, axis=0) for v in x]
        x = [jnp.maximum(x[k], y[TOPK - 1 - k]) for k in range(TOPK)]
        d = TOPK // 2
        while d >= 1:
            for k in range(TOPK):
                if k & d == 0:
                    _exchange(x, k, k + d)
            d //= 2
        shift //= 2
    return [v[0:1, :] for v in x]


def _key_slabs(s):
    return [s[SUBLANES * g:SUBLANES * (g + 1), :] for g in range(s.shape[0] // SUBLANES)]


def _peer_select(st_ref, c_ref, e0_ref, r1_ref, e1_ref):
    for h in range(PK_HEADS):
        s0 = st_ref[2 * h]
        s1 = st_ref[2 * h + 1]
        a_rows = _top16_sorted(_key_slabs(s0))
        b_rows = _top16_sorted(_key_slabs(s1))
        a = jnp.concatenate(a_rows, axis=0)
        b = jnp.concatenate(b_rows, axis=0)
        half = TOPK // 2
        cand = [a_rows[0] + b[0:half], a_rows[0] + b[half:]]
        cand += [a_rows[ra] + b[0:half] for ra in range(1, half)]
        cand.append(a[half:] + b_rows[0])
        cand += [jnp.full_like(cand[0], MASKED)] * (TOPK - len(cand))
        tau = _top16_sorted(cand)[TOPK - 1]
        ea = jnp.exp(a - a_rows[0])
        eb = jnp.exp(b - b_rows[0])
        z = jnp.zeros_like(tau)
        cnt = jnp.zeros_like(s0)
        for k in range(TOPK):
            hit = a + b_rows[k] >= tau
            z = z + eb[k:k + 1] * jnp.sum(jnp.where(hit, ea, 0.0), axis=0, keepdims=True)
            reach = jnp.min(jnp.where(hit, a, -MASKED), axis=0, keepdims=True)
            cnt = jnp.where(s0 >= reach, float(k + 1), cnt)
        rank1 = jnp.full_like(s1, float(TOPK))
        for k in reversed(range(TOPK)):
            rank1 = jnp.where(s1 >= b_rows[k], float(k), rank1)
        e0 = jnp.exp(s0 - a_rows[0]) * (0.5 / z)
        for q in range(s0.shape[1] // BLOCK):
            c_ref[h, q] = cnt[:, q * BLOCK:(q + 1) * BLOCK]
            e0_ref[h, q] = e0[:, q * BLOCK:(q + 1) * BLOCK]
        r1_ref[h] = rank1.astype(BF16)
        e1_ref[h] = jnp.exp(s1 - b[0:1]).astype(BF16)


def _peer_kernel(st_ref, xbt_ref, h1_ref, u_ref, vt_ref, fg_ref, out_ref,
                 c_ref, e0_ref, r1_ref, e1_ref, acc_ref, crow_ref, e0row_ref,
                 at0_ref, at1_ref, w0_ref, w1_ref):
    e = pl.program_id(1)
    n_e = pl.num_programs(1)
    n_pair = u_ref.shape[0]
    PAIR = u_ref.shape[1]
    TB = xbt_ref.shape[1]

    @pl.when(e == 0)
    def _():
        acc_ref[...] = jnp.zeros_like(acc_ref)
        _peer_select(st_ref, c_ref, e0_ref, r1_ref, e1_ref)

    n_row = 2 * n_pair
    n_q = TB // BLOCK
    base = pl.multiple_of(e * n_row, n_row)
    for h in range(PK_HEADS):
        for q in range(n_q):
            crow_ref[h, q] = c_ref[h, q, pl.ds(base, n_row), :]
            e0row_ref[h, q] = e0_ref[h, q, pl.ds(base, n_row), :]

    def pre_act(p, dst_ref):
        dst_ref[...] = jnp.dot(u_ref[p], xbt_ref[...], preferred_element_type=F32)

    def weights(p, src_ref, dst_ref):
        r_out = (p % 2) * PAIR
        for q in range(n_q):
            lanes = slice(q * BLOCK, (q + 1) * BLOCK)
            gates = [jnp.zeros((N_KEYS, BLOCK), BF16) for _ in range(2)]
            for h in range(PK_HEADS):
                r1 = r1_ref[h, :, lanes]
                e1 = e1_ref[h, :, lanes]
                for d in range(2):
                    i = 2 * p + d
                    cnt = crow_ref[h, q, i:i + 1, :].astype(BF16)
                    e0 = e0row_ref[h, q, i:i + 1, :].astype(BF16)
                    gates[d] = gates[d] + jnp.where(r1 < cnt, e1, jnp.zeros((), BF16)) * e0
            for d in range(2):
                rows = slice(d * N_KEYS, (d + 1) * N_KEYS)
                a = src_ref[rows, lanes]
                act = a * (1.0 + lax.erf(a * math.sqrt(0.5)))
                dst_ref[r_out + d * N_KEYS:r_out + (d + 1) * N_KEYS, lanes] = (
                    act.astype(BF16) * gates[d])

    def expert_out(pp, src_ref):
        acc_ref[...] += jnp.dot(vt_ref[pp], src_ref[...], preferred_element_type=F32)

    at_refs = (at0_ref, at1_ref)
    w_refs = (w0_ref, w1_ref)
    pre_act(0, at_refs[0])
    for p in range(n_pair):
        if p + 1 < n_pair:
            pre_act(p + 1, at_refs[(p + 1) % 2])
        weights(p, at_refs[p % 2], w_refs[(p // 2) % 2])
        if p % 2 == 0 and p >= 2:
            expert_out(p // 2 - 1, w_refs[(p // 2 - 1) % 2])
    expert_out(n_pair // 2 - 1, w_refs[(n_pair // 2 - 1) % 2])

    @pl.when(e == n_e - 1)
    def _():
        out_ref[...] = _rms(h1_ref[...] + acc_ref[...].T, fg_ref[...])


def _tile(total, pref):
    t = pref
    while total % t:
        t //= 2
    return t


def _const_spec(shape):
    return pl.BlockSpec(shape, lambda *_: (0,) * len(shape))


def kernel(x, meta_tokens, mix_norm_g, w_in, b_gate, hg_lb_logits, hg_norm_g, attn_sinks,
           w_branch_hg, w_branch_attn, w_out, ffn_norm_g, peer_w_q, peer_subkeys,
           peer_u, peer_v, final_norm_g):
    B, S, D = x.shape
    assert S % BLOCK == 0 and w_in.shape[0] == 1
    P = S + BLOCK
    NB = P // BLOCK
    T = B * P
    n_exp = peer_u.shape[1]

    h0 = jnp.concatenate(
        [jnp.zeros((B, N_PAD, D), x.dtype),
         jnp.broadcast_to(meta_tokens[None].astype(x.dtype), (B, N_META, D)), x],
        axis=1).reshape(T, D)

    w_in_b = w_in[0].astype(BF16)
    in_cols = w_in_b.shape[1]
    c_hg = 4 * HG_WIDTH
    c_at = AT_WIDTH + 2 * AT_KV_WIDTH
    c_g = 2 * D

    TM = _tile(T, 256)
    z_hg, z_at, z_g = pl.pallas_call(
        _inproj_kernel,
        grid=(T // TM,),
        in_specs=[pl.BlockSpec((TM, D), lambda i: (i, 0)),
                  _const_spec((1, D)),
                  _const_spec((D, in_cols))],
        out_specs=[pl.BlockSpec((TM, c_hg), lambda i: (i, 0)),
                   pl.BlockSpec((TM, c_at), lambda i: (i, 0)),
                   pl.BlockSpec((TM, c_g), lambda i: (i, 0))],
        out_shape=[jax.ShapeDtypeStruct((T, c_hg), F32),
                   jax.ShapeDtypeStruct((T, c_at), F32),
                   jax.ShapeDtypeStruct((T, c_g), F32)],
        compiler_params=pltpu.CompilerParams(dimension_semantics=("parallel",),
                                             vmem_limit_bytes=VMEM_LIMIT),
        name="inproj",
    )(h0, mix_norm_g[0:1], w_in_b)

    o_hg = pl.pallas_call(
        _hgrn_kernel,
        grid=(B, NB),
        in_specs=[pl.BlockSpec((BLOCK, c_hg), lambda b, n: (b * NB + n, 0)),
                  _const_spec((2, HG_WIDTH)),
                  _const_spec((1, HG_WIDTH))],
        out_specs=pl.BlockSpec((BLOCK, HG_WIDTH), lambda b, n: (b * NB + n, 0)),
        out_shape=jax.ShapeDtypeStruct((T, HG_WIDTH), F32),
        scratch_shapes=[pltpu.VMEM((HG_HEADS, HG_DK, HG_DK), F32)],
        compiler_params=pltpu.CompilerParams(dimension_semantics=("parallel", "arbitrary"),
                                             vmem_limit_bytes=VMEM_LIMIT),
        name="hgrn",
    )(z_hg, hg_lb_logits.astype(F32), hg_norm_g[0:1])

    BS = 1
    z_at3 = z_at.reshape(B, P, c_at)
    o_at = pl.pallas_call(
        _swa_kernel,
        grid=(B // BS, NB),
        in_specs=[pl.BlockSpec(memory_space=pltpu.SMEM),
                  pl.BlockSpec((BS, BLOCK, c_at), lambda b, n: (b, n, 0)),
                  pl.BlockSpec((BS, BLOCK, c_at), lambda b, n: (b, jnp.maximum(n - 1, 0), 0)),
                  pl.BlockSpec((BS, BLOCK, c_at), lambda b, n: (b, 0, 0))],
        out_specs=pl.BlockSpec((BS, BLOCK, AT_WIDTH), lambda b, n: (b, n, 0)),
        out_shape=jax.ShapeDtypeStruct((B, P, AT_WIDTH), F32),
        compiler_params=pltpu.CompilerParams(dimension_semantics=("parallel", "arbitrary"),
                                             vmem_limit_bytes=VMEM_LIMIT),
        name="swa",
    )(attn_sinks[0].astype(F32), z_at3, z_at3, z_at3).reshape(T, AT_WIDTH)

    n_hc = 2 * PK_HEADS
    sk = peer_subkeys[0].reshape(n_hc, N_KEYS, -1).astype(BF16)
    d_half = sk.shape[-1]
    TM4 = _tile(T, 256)
    h1, xbt, st = pl.pallas_call(
        _merge_kernel,
        grid=(T // TM4,),
        in_specs=[pl.BlockSpec((TM4, D), lambda i: (i, 0)),
                  pl.BlockSpec((TM4, HG_WIDTH), lambda i: (i, 0)),
                  pl.BlockSpec((TM4, AT_WIDTH), lambda i: (i, 0)),
                  pl.BlockSpec((TM4, c_g), lambda i: (i, 0)),
                  _const_spec((1, c_g)),
                  _const_spec((HG_WIDTH, D)),
                  _const_spec((AT_WIDTH, D)),
                  _const_spec((D, D)),
                  _const_spec((1, D)),
                  _const_spec((D, n_hc * d_half)),
                  _const_spec((n_hc, N_KEYS, d_half))],
        out_specs=[pl.BlockSpec((TM4, D), lambda i: (i, 0)),
                   pl.BlockSpec((D, TM4), lambda i: (0, i)),
                   pl.BlockSpec((n_hc, N_KEYS, TM4), lambda i: (0, 0, i))],
        out_shape=[jax.ShapeDtypeStruct((T, D), F32),
                   jax.ShapeDtypeStruct((D, T), BF16),
                   jax.ShapeDtypeStruct((n_hc, N_KEYS, T), F32)],
        compiler_params=pltpu.CompilerParams(dimension_semantics=("parallel",),
                                             vmem_limit_bytes=VMEM_LIMIT),
        name="merge",
    )(h0, o_hg, o_at, z_g, b_gate[0:1], w_branch_hg[0].astype(BF16),
      w_branch_attn[0].astype(BF16), w_out[0].astype(BF16), ffn_norm_g[0:1],
      peer_w_q[0].astype(BF16), sk)

    TB = _tile(T, 1024)
    PAIR = 2 * N_KEYS
    n_pairs = n_exp // PAIR
    PB = 4
    u_b = peer_u[0].astype(BF16).reshape(n_pairs, PAIR, D)
    vt_b = peer_v[0].astype(BF16).reshape(n_pairs // 2, 2 * PAIR, D).transpose(0, 2, 1)
    out = pl.pallas_call(
        _peer_kernel,
        grid=(T // TB, n_pairs // PB),
        in_specs=[pl.BlockSpec((n_hc, N_KEYS, TB), lambda i, e: (0, 0, i),
                               pipeline_mode=pl.Buffered(1)),
                  pl.BlockSpec((D, TB), lambda i, e: (0, i)),
                  pl.BlockSpec((TB, D), lambda i, e: (i, 0), pipeline_mode=pl.Buffered(1)),
                  pl.BlockSpec((PB, PAIR, D), lambda i, e: (e, 0, 0)),
                  pl.BlockSpec((PB // 2, D, 2 * PAIR), lambda i, e: (e, 0, 0)),
                  _const_spec((1, D))],
        out_specs=pl.BlockSpec((TB, D), lambda i, e: (i, 0)),
        out_shape=jax.ShapeDtypeStruct((T, D), F32),
        scratch_shapes=[pltpu.VMEM((PK_HEADS, TB // BLOCK, N_KEYS, BLOCK), F32),
                        pltpu.VMEM((PK_HEADS, TB // BLOCK, N_KEYS, BLOCK), F32),
                        pltpu.VMEM((PK_HEADS, N_KEYS, TB), BF16),
                        pltpu.VMEM((PK_HEADS, N_KEYS, TB), BF16),
                        pltpu.VMEM((D, TB), F32),
                        pltpu.VMEM((PK_HEADS, TB // BLOCK, 2 * PB, BLOCK), F32),
                        pltpu.VMEM((PK_HEADS, TB // BLOCK, 2 * PB, BLOCK), F32),
                        pltpu.VMEM((PAIR, TB), F32),
                        pltpu.VMEM((PAIR, TB), F32),
                        pltpu.VMEM((2 * PAIR, TB), BF16),
                        pltpu.VMEM((2 * PAIR, TB), BF16)],
        compiler_params=pltpu.CompilerParams(dimension_semantics=("parallel", "arbitrary"),
                                             vmem_limit_bytes=58 * 1024 * 1024),
        name="peer",
    )(st, xbt, h1, u_b, vt_b, final_norm_g.reshape(1, D))

    return out.reshape(B, P, D)[:, BLOCK:]
```

```python
import functools
import math

import jax
import jax.numpy as jnp
from jax import lax
from jax.experimental import pallas as pl
from jax.experimental.pallas import tpu as pltpu

N_META = 16
BLOCK = 128
N_PAD = BLOCK - N_META
HG_HEADS = 4
HG_DK = 128
HG_WIDTH = HG_HEADS * HG_DK
AT_QHEADS = 8
AT_KVHEADS = 2
AT_GROUP = AT_QHEADS // AT_KVHEADS
AT_HDIM = 64
AT_WIDTH = AT_QHEADS * AT_HDIM
AT_KV_WIDTH = AT_KVHEADS * AT_HDIM
PK_HEADS = 8
N_KEYS = 128
TOPK = 16
EPS = 1e-6
MASKED = -1e30
VMEM_LIMIT = 48 * 1024 * 1024

F32 = jnp.float32
BF16 = jnp.bfloat16


def _dot(a, b):
    return jnp.dot(a.astype(BF16), b.astype(BF16), preferred_element_type=F32)


def _dot_nt(a, b):
    return lax.dot_general(a.astype(BF16), b.astype(BF16), (((1,), (1,)), ((), ())),
                           preferred_element_type=F32)


def _rms(x, g):
    return x * lax.rsqrt(jnp.mean(x * x, axis=-1, keepdims=True) + EPS) * g


def _inproj_kernel(h_ref, g_ref, w_ref, zhg_ref, zat_ref, zg_ref):
    yb = _rms(h_ref[...], g_ref[...]).astype(BF16)
    c0 = 4 * HG_WIDTH
    c1 = c0 + AT_WIDTH + 2 * AT_KV_WIDTH
    zhg_ref[...] = jnp.dot(yb, w_ref[:, 0:c0], preferred_element_type=F32)
    zat_ref[...] = jnp.dot(yb, w_ref[:, c0:c1], preferred_element_type=F32)
    zg_ref[...] = jnp.dot(yb, w_ref[:, c1:], preferred_element_type=F32)


def _hgrn_kernel(z_ref, lbl_ref, ng_ref, o_ref, st_ref):
    n = pl.program_id(1)

    @pl.when(n == 0)
    def _():
        st_ref[...] = jnp.zeros_like(st_ref)

    W = HG_WIDTH
    q = z_ref[:, 0:W]
    fl = z_ref[:, W:2 * W]
    v = z_ref[:, 2 * W:3 * W]
    gate = z_ref[:, 3 * W:4 * W]

    l = lbl_ref[...]
    e = jnp.exp(l - jnp.max(l, axis=0, keepdims=True))
    lb = e[0:1] / jnp.sum(e, axis=0, keepdims=True)

    f = lb + (1.0 - lb) * jax.nn.sigmoid(fl)
    row = lax.broadcasted_iota(jnp.int32, (BLOCK, W), 0)
    live = jnp.logical_or(n > 0, row >= N_PAD)
    logf = jnp.where(live, jnp.log(f), 0.0)
    k = jnp.where(live, 1.0 - f, 0.0)

    A = logf
    s = 1
    while s < BLOCK:
        A = A + jnp.where(row >= s, pltpu.roll(A, s, axis=0), 0.0)
        s *= 2

    q_lv = [q]
    k_lv = [k]
    piv_q = jnp.where(row >= 1, pltpu.roll(A, 1, axis=0), 0.0)
    piv_k = A
    m = 1
    while m < BLOCK:
        bit = (row & m) != 0
        q_lv.append(q * jnp.exp(A - piv_q))
        k_lv.append(k * jnp.exp(piv_k - A))
        piv_q = jnp.where(bit, pltpu.roll(piv_q, m, axis=0), piv_q)
        piv_k = jnp.where(bit, piv_k, pltpu.roll(piv_k, BLOCK - m, axis=0))
        m *= 2
    q_in = q * jnp.exp(A)
    k_out = k * jnp.exp(piv_k - A)
    a_end = piv_k[0:1, :]

    ti = lax.broadcasted_iota(jnp.int32, (BLOCK, BLOCK), 0)
    si = lax.broadcasted_iota(jnp.int32, (BLOCK, BLOCK), 1)
    tx = ti ^ si
    ng = ng_ref[...]
    for hh in range(HG_HEADS):
        sl = slice(hh * HG_DK, (hh + 1) * HG_DK)
        scores = jnp.where(ti == si, _dot_nt(q_lv[0][:, sl], k_lv[0][:, sl]), 0.0)
        m = 1
        lv = 1
        while m < BLOCK:
            mask = jnp.logical_and((tx >> (lv - 1)) == 1, (ti & m) != 0)
            scores = scores + jnp.where(mask, _dot_nt(q_lv[lv][:, sl], k_lv[lv][:, sl]), 0.0)
            m *= 2
            lv += 1
        st = st_ref[hh]
        o = _dot(scores, v[:, sl]) + _dot_nt(q_in[:, sl], st)
        st_ref[hh] = st * jnp.exp(a_end[:, sl]) + _dot(v[:, sl].T, k_out[:, sl])
        o = _rms(o, ng[:, sl])
        g = gate[:, sl]
        o_ref[:, sl] = o * (g * jax.nn.sigmoid(g))


def _swa_kernel(sink_ref, cur_ref, prev_ref, first_ref, o_ref):
    n = pl.program_id(1)
    R = AT_GROUP * BLOCK
    NK = 3 * BLOCK
    r = lax.broadcasted_iota(jnp.int32, (R, NK), 0) & (BLOCK - 1)
    c = lax.broadcasted_iota(jnp.int32, (R, NK), 1)
    qpos = n * BLOCK + r
    band = (c > r) & (c <= r + BLOCK) & ((n - 1) * BLOCK + c >= BLOCK) & (c < 2 * BLOCK)
    cm = c - 2 * BLOCK
    meta = (cm >= N_PAD) & (cm <= qpos)
    mask = band | meta
    neg = jnp.finfo(F32).min
    scale = AT_HDIM ** -0.5
    ko = AT_WIDTH
    vo = AT_WIDTH + AT_KV_WIDTH
    for bi in range(cur_ref.shape[0]):
        for g in range(AT_KVHEADS):
            ks = slice(ko + g * AT_HDIM, ko + (g + 1) * AT_HDIM)
            vs = slice(vo + g * AT_HDIM, vo + (g + 1) * AT_HDIM)
            kall = jnp.concatenate([prev_ref[bi, :, ks], cur_ref[bi, :, ks], first_ref[bi, :, ks]],
                                   axis=0)
            vall = jnp.concatenate([prev_ref[bi, :, vs], cur_ref[bi, :, vs], first_ref[bi, :, vs]],
                                   axis=0)
            qs = jnp.concatenate(
                [cur_ref[bi, :, (g * AT_GROUP + j) * AT_HDIM:(g * AT_GROUP + j + 1) * AT_HDIM]
                 for j in range(AT_GROUP)], axis=0)
            sink = jnp.concatenate(
                [jnp.full((BLOCK, 1), sink_ref[g * AT_GROUP + j], F32) for j in range(AT_GROUP)],
                axis=0)
            sc = jnp.where(mask, _dot_nt(qs, kall) * scale, neg)
            mx = jnp.maximum(jnp.max(sc, axis=-1, keepdims=True), sink)
            p = jnp.exp(sc - mx)
            den = jnp.sum(p, axis=-1, keepdims=True) + jnp.exp(sink - mx)
            o = _dot(p, vall) / den
            for j in range(AT_GROUP):
                hd = g * AT_GROUP + j
                o_ref[bi, :, hd * AT_HDIM:(hd + 1) * AT_HDIM] = o[j * BLOCK:(j + 1) * BLOCK, :]


def _merge_kernel(h_ref, ohg_ref, oat_ref, zg_ref, bg_ref, wbh_ref, wba_ref, wo_ref, fg_ref,
                  wq_ref, sk_ref, h1_ref, xb_ref, st_ref):
    D = h_ref.shape[-1]
    gl = zg_ref[...] + bg_ref[...]
    g_hg = jax.nn.sigmoid(gl[:, :D])
    g_at = jax.nn.sigmoid(gl[:, D:])
    mrg = (g_hg * jnp.dot(ohg_ref[...].astype(BF16), wbh_ref[...], preferred_element_type=F32)
           + g_at * jnp.dot(oat_ref[...].astype(BF16), wba_ref[...], preferred_element_type=F32))
    h1 = h_ref[...] + jnp.dot(mrg.astype(BF16), wo_ref[...], preferred_element_type=F32)
    h1_ref[...] = h1
    xn = _rms(h1, fg_ref[...])
    xb = xn.astype(BF16)
    xb_ref[...] = xn.T.astype(BF16)
    qp = jnp.dot(xb, wq_ref[...], preferred_element_type=F32).astype(BF16)
    for hc in range(2 * PK_HEADS):
        st_ref[hc] = lax.dot_general(sk_ref[hc], qp[:, hc * N_KEYS:(hc + 1) * N_KEYS],
                                     (((1,), (1,)), ((), ())), preferred_element_type=F32)


SUBLANES = 8


def _oddeven_mergesort_pairs(n):
    pairs = []
    p = 1
    while p < n:
        k = p
        while k >= 1:
            for j in range(k % p, n - k, 2 * k):
                for i in range(min(k, n - j - k)):
                    if (i + j) // (2 * p) == (i + j + k) // (2 * p):
                        pairs.append((i + j, i + j + k))
            k //= 2
        p *= 2
    return pairs


def _exchange(x, i, j):
    x[i], x[j] = jnp.maximum(x[i], x[j]), jnp.minimum(x[i], x[j])


def _top16_sorted(slabs):
    x = list(slabs)
    for i, j in _oddeven_mergesort_pairs(TOPK):
        _exchange(x, i, j)
    shift = SUBLANES // 2
    while shift >= 1:
        y = [pltpu.roll(v, shift, axis=0) for v in x]
        x = [jnp.maximum(x[k], y[TOPK - 1 - k]) for k in range(TOPK)]
        d = TOPK // 2
        while d >= 1:
            for k in range(TOPK):
                if k & d == 0:
                    _exchange(x, k, k + d)
            d //= 2
        shift //= 2
    return [v[0:1, :] for v in x]


def _key_slabs(s):
    return [s[SUBLANES * g:SUBLANES * (g + 1), :] for g in range(s.shape[0] // SUBLANES)]


def _peer_select(st_ref, c_ref, e0_ref, r1_ref, e1_ref):
    for h in range(PK_HEADS):
        s0 = st_ref[2 * h]
        s1 = st_ref[2 * h + 1]
        a_rows = _top16_sorted(_key_slabs(s0))
        b_rows = _top16_sorted(_key_slabs(s1))
        a = jnp.concatenate(a_rows, axis=0)
        b = jnp.concatenate(b_rows, axis=0)
        half = TOPK // 2
        cand = [a_rows[0] + b[0:half], a_rows[0] + b[half:]]
        cand += [a_rows[ra] + b[0:half] for ra in range(1, half)]
        cand.append(a[half:] + b_rows[0])
        cand += [jnp.full_like(cand[0], MASKED)] * (TOPK - len(cand))
        tau = _top16_sorted(cand)[TOPK - 1]
        ea = jnp.exp(a - a_rows[0])
        eb = jnp.exp(b - b_rows[0])
        z = jnp.zeros_like(tau)
        cnt = jnp.zeros_like(s0)
        for k in range(TOPK):
            hit = a + b_rows[k] >= tau
            z = z + eb[k:k + 1] * jnp.sum(jnp.where(hit, ea, 0.0), axis=0, keepdims=True)
            reach = jnp.min(jnp.where(hit, a, -MASKED), axis=0, keepdims=True)
            cnt = jnp.where(s0 >= reach, float(k + 1), cnt)
        rank1 = jnp.full_like(s1, float(TOPK))
        for k in reversed(range(TOPK)):
            rank1 = jnp.where(s1 >= b_rows[k], float(k), rank1)
        e0 = jnp.exp(s0 - a_rows[0]) * (0.5 / z)
        e1 = jnp.exp(s1 - b_rows[0])
        for q in range(s0.shape[1] // BLOCK):
            lanes = slice(q * BLOCK, (q + 1) * BLOCK)
            c_ref[h, q] = cnt[:, lanes]
            e0_ref[h, q] = e0[:, lanes]
            r1_ref[h, q] = rank1[:, lanes].astype(BF16)
            e1_ref[h, q] = e1[:, lanes].astype(BF16)


def _peer_kernel(st_ref, xbt_ref, h1_ref, u_ref, vt_ref, fg_ref, out_ref,
                 c_ref, e0_ref, r1_ref, e1_ref, acc_ref, crow_ref, e0row_ref,
                 at0_ref, at1_ref, w0_ref, w1_ref):
    e = pl.program_id(1)
    n_e = pl.num_programs(1)
    n_grp = u_ref.shape[0]
    PAIR = 2 * N_KEYS
    pairs_per_grp = u_ref.shape[1] // PAIR
    n_pair = n_grp * pairs_per_grp
    TB = xbt_ref.shape[1]

    @pl.when(e == 0)
    def _():
        acc_ref[...] = jnp.zeros_like(acc_ref)
        _peer_select(st_ref, c_ref, e0_ref, r1_ref, e1_ref)

    n_row = 2 * n_pair
    n_q = TB // BLOCK
    base = pl.multiple_of(e * n_row, n_row)
    for h in range(PK_HEADS):
        for q in range(n_q):
            crow_ref[h, q] = c_ref[h, q, pl.ds(base, n_row), :]
            e0row_ref[h, q] = e0_ref[h, q, pl.ds(base, n_row), :]

    def pre_act(g, dst_ref):
        res = jnp.dot(u_ref[g], xbt_ref[...], preferred_element_type=F32)
        for q in range(n_q):
            dst_ref[q] = res[:, q * BLOCK:(q + 1) * BLOCK]

    def weights(p, src_ref, dst_ref):
        r0 = (p % pairs_per_grp) * PAIR
        for q in range(n_q):
            lanes = slice(q * BLOCK, (q + 1) * BLOCK)
            gates = [jnp.zeros((N_KEYS, BLOCK), BF16) for _ in range(2)]
            for h in range(PK_HEADS):
                r1 = r1_ref[h, q]
                e1 = e1_ref[h, q]
                for d in range(2):
                    i = 2 * p + d
                    cnt = crow_ref[h, q, i:i + 1, :].astype(BF16)
                    e0 = e0row_ref[h, q, i:i + 1, :].astype(BF16)
                    gates[d] = gates[d] + jnp.where(r1 < cnt, e1, jnp.zeros((), BF16)) * e0
            for d in range(2):
                rows = slice(r0 + d * N_KEYS, r0 + (d + 1) * N_KEYS)
                a = src_ref[q, rows, :]
                act = a * (1.0 + lax.erf(a * math.sqrt(0.5)))
                dst_ref[q, rows, :] = act.astype(BF16) * gates[d]

    def expert_out(g, src_ref):
        w = jnp.concatenate([src_ref[q] for q in range(n_q)], axis=1)
        acc_ref[...] += jnp.dot(vt_ref[g], w, preferred_element_type=F32)

    at_refs = (at0_ref, at1_ref)
    w_refs = (w0_ref, w1_ref)
    pre_act(0, at_refs[0])
    for g in range(n_grp):
        if g + 1 < n_grp:
            pre_act(g + 1, at_refs[(g + 1) % 2])
        for k in range(pairs_per_grp):
            weights(g * pairs_per_grp + k, at_refs[g % 2], w_refs[g % 2])
        if g > 0:
            expert_out(g - 1, w_refs[(g - 1) % 2])
    expert_out(n_grp - 1, w_refs[(n_grp - 1) % 2])

    @pl.when(e == n_e - 1)
    def _():
        out_ref[...] = _rms(h1_ref[...] + acc_ref[...].T, fg_ref[...])


def _tile(total, pref):
    t = pref
    while total % t:
        t //= 2
    return t


def _const_spec(shape):
    return pl.BlockSpec(shape, lambda *_: (0,) * len(shape))


def kernel(x, meta_tokens, mix_norm_g, w_in, b_gate, hg_lb_logits, hg_norm_g, attn_sinks,
           w_branch_hg, w_branch_attn, w_out, ffn_norm_g, peer_w_q, peer_subkeys,
           peer_u, peer_v, final_norm_g):
    B, S, D = x.shape
    assert S % BLOCK == 0 and w_in.shape[0] == 1
    P = S + BLOCK
    NB = P // BLOCK
    T = B * P
    n_exp = peer_u.shape[1]

    h0 = jnp.concatenate(
        [jnp.zeros((B, N_PAD, D), x.dtype),
         jnp.broadcast_to(meta_tokens[None].astype(x.dtype), (B, N_META, D)), x],
        axis=1).reshape(T, D)

    w_in_b = w_in[0].astype(BF16)
    in_cols = w_in_b.shape[1]
    c_hg = 4 * HG_WIDTH
    c_at = AT_WIDTH + 2 * AT_KV_WIDTH
    c_g = 2 * D

    TM = _tile(T, 256)
    z_hg, z_at, z_g = pl.pallas_call(
        _inproj_kernel,
        grid=(T // TM,),
        in_specs=[pl.BlockSpec((TM, D), lambda i: (i, 0)),
                  _const_spec((1, D)),
                  _const_spec((D, in_cols))],
        out_specs=[pl.BlockSpec((TM, c_hg), lambda i: (i, 0)),
                   pl.BlockSpec((TM, c_at), lambda i: (i, 0)),
                   pl.BlockSpec((TM, c_g), lambda i: (i, 0))],
        out_shape=[jax.ShapeDtypeStruct((T, c_hg), F32),
                   jax.ShapeDtypeStruct((T, c_at), F32),
                   jax.ShapeDtypeStruct((T, c_g), F32)],
        compiler_params=pltpu.CompilerParams(dimension_semantics=("parallel",),
                                             vmem_limit_bytes=VMEM_LIMIT),
        name="inproj",
    )(h0, mix_norm_g[0:1], w_in_b)

    o_hg = pl.pallas_call(
        _hgrn_kernel,
        grid=(B, NB),
        in_specs=[pl.BlockSpec((BLOCK, c_hg), lambda b, n: (b * NB + n, 0)),
                  _const_spec((2, HG_WIDTH)),
                  _const_spec((1, HG_WIDTH))],
        out_specs=pl.BlockSpec((BLOCK, HG_WIDTH), lambda b, n: (b * NB + n, 0)),
        out_shape=jax.ShapeDtypeStruct((T, HG_WIDTH), F32),
        scratch_shapes=[pltpu.VMEM((HG_HEADS, HG_DK, HG_DK), F32)],
        compiler_params=pltpu.CompilerParams(dimension_semantics=("parallel", "arbitrary"),
                                             vmem_limit_bytes=VMEM_LIMIT),
        name="hgrn",
    )(z_hg, hg_lb_logits.astype(F32), hg_norm_g[0:1])

    BS = 1
    z_at3 = z_at.reshape(B, P, c_at)
    o_at = pl.pallas_call(
        _swa_kernel,
        grid=(B // BS, NB),
        in_specs=[pl.BlockSpec(memory_space=pltpu.SMEM),
                  pl.BlockSpec((BS, BLOCK, c_at), lambda b, n: (b, n, 0)),
                  pl.BlockSpec((BS, BLOCK, c_at), lambda b, n: (b, jnp.maximum(n - 1, 0), 0)),
                  pl.BlockSpec((BS, BLOCK, c_at), lambda b, n: (b, 0, 0))],
        out_specs=pl.BlockSpec((BS, BLOCK, AT_WIDTH), lambda b, n: (b, n, 0)),
        out_shape=jax.ShapeDtypeStruct((B, P, AT_WIDTH), F32),
        compiler_params=pltpu.CompilerParams(dimension_semantics=("parallel", "arbitrary"),
                                             vmem_limit_bytes=VMEM_LIMIT),
        name="swa",
    )(attn_sinks[0].astype(F32), z_at3, z_at3, z_at3).reshape(T, AT_WIDTH)

    n_hc = 2 * PK_HEADS
    sk = peer_subkeys[0].reshape(n_hc, N_KEYS, -1).astype(BF16)
    d_half = sk.shape[-1]
    TM4 = _tile(T, 256)
    h1, xbt, st = pl.pallas_call(
        _merge_kernel,
        grid=(T // TM4,),
        in_specs=[pl.BlockSpec((TM4, D), lambda i: (i, 0)),
                  pl.BlockSpec((TM4, HG_WIDTH), lambda i: (i, 0)),
                  pl.BlockSpec((TM4, AT_WIDTH), lambda i: (i, 0)),
                  pl.BlockSpec((TM4, c_g), lambda i: (i, 0)),
                  _const_spec((1, c_g)),
                  _const_spec((HG_WIDTH, D)),
                  _const_spec((AT_WIDTH, D)),
                  _const_spec((D, D)),
                  _const_spec((1, D)),
                  _const_spec((D, n_hc * d_half)),
                  _const_spec((n_hc, N_KEYS, d_half))],
        out_specs=[pl.BlockSpec((TM4, D), lambda i: (i, 0)),
                   pl.BlockSpec((D, TM4), lambda i: (0, i)),
                   pl.BlockSpec((n_hc, N_KEYS, TM4), lambda i: (0, 0, i))],
        out_shape=[jax.ShapeDtypeStruct((T, D), F32),
                   jax.ShapeDtypeStruct((D, T), BF16),
                   jax.ShapeDtypeStruct((n_hc, N_KEYS, T), F32)],
        compiler_params=pltpu.CompilerParams(dimension_semantics=("parallel",),
                                             vmem_limit_bytes=VMEM_LIMIT),
        name="merge",
    )(h0, o_hg, o_at, z_g, b_gate[0:1], w_branch_hg[0].astype(BF16),
      w_branch_attn[0].astype(BF16), w_out[0].astype(BF16), ffn_norm_g[0:1],
      peer_w_q[0].astype(BF16), sk)

    TB = _tile(T, 512)
    PAIR = 2 * N_KEYS
    PB = 8
    GRP = 2 * PAIR
    n_grps = n_exp // GRP
    GB = PB * PAIR // GRP
    u_b = peer_u[0].astype(BF16).reshape(n_grps, GRP, D)
    vt_b = peer_v[0].astype(BF16).reshape(n_grps, GRP, D).transpose(0, 2, 1)
    out = pl.pallas_call(
        _peer_kernel,
        grid=(T // TB, n_grps // GB),
        in_specs=[pl.BlockSpec((n_hc, N_KEYS, TB), lambda i, e: (0, 0, i)),
                  pl.BlockSpec((D, TB), lambda i, e: (0, i)),
                  pl.BlockSpec((TB, D), lambda i, e: (i, 0)),
                  pl.BlockSpec((GB, GRP, D), lambda i, e: (e, 0, 0)),
                  pl.BlockSpec((GB, D, GRP), lambda i, e: (e, 0, 0)),
                  _const_spec((1, D))],
        out_specs=pl.BlockSpec((TB, D), lambda i, e: (i, 0)),
        out_shape=jax.ShapeDtypeStruct((T, D), F32),
        scratch_shapes=[pltpu.VMEM((PK_HEADS, TB // BLOCK, N_KEYS, BLOCK), F32),
                        pltpu.VMEM((PK_HEADS, TB // BLOCK, N_KEYS, BLOCK), F32),
                        pltpu.VMEM((PK_HEADS, TB // BLOCK, N_KEYS, BLOCK), BF16),
                        pltpu.VMEM((PK_HEADS, TB // BLOCK, N_KEYS, BLOCK), BF16),
                        pltpu.VMEM((D, TB), F32),
                        pltpu.VMEM((PK_HEADS, TB // BLOCK, 2 * PB, BLOCK), F32),
                        pltpu.VMEM((PK_HEADS, TB // BLOCK, 2 * PB, BLOCK), F32),
                        pltpu.VMEM((TB // BLOCK, GRP, BLOCK), F32),
                        pltpu.VMEM((TB // BLOCK, GRP, BLOCK), F32),
                        pltpu.VMEM((TB // BLOCK, GRP, BLOCK), BF16),
                        pltpu.VMEM((TB // BLOCK, GRP, BLOCK), BF16)],
        compiler_params=pltpu.CompilerParams(dimension_semantics=("parallel", "arbitrary"),
                                             vmem_limit_bytes=56 * 1024 * 1024),
        name="peer",
    )(st, xbt, h1, u_b, vt_b, final_norm_g.reshape(1, D))

    return out.reshape(B, P, D)[:, BLOCK:]
```

```python
import functools
import math

import jax
import jax.numpy as jnp
from jax import lax
from jax.experimental import pallas as pl
from jax.experimental.pallas import tpu as pltpu

N_META = 16
BLOCK = 128
N_PAD = BLOCK - N_META
HG_HEADS = 4
HG_DK = 128
HG_WIDTH = HG_HEADS * HG_DK
AT_QHEADS = 8
AT_KVHEADS = 2
AT_GROUP = AT_QHEADS // AT_KVHEADS
AT_HDIM = 64
AT_WIDTH = AT_QHEADS * AT_HDIM
AT_KV_WIDTH = AT_KVHEADS * AT_HDIM
PK_HEADS = 8
N_KEYS = 128
TOPK = 16
EPS = 1e-6
MASKED = -1e30
VMEM_LIMIT = 48 * 1024 * 1024

F32 = jnp.float32
BF16 = jnp.bfloat16


def _dot(a, b):
    return jnp.dot(a.astype(BF16), b.astype(BF16), preferred_element_type=F32)


def _dot_nt(a, b):
    return lax.dot_general(a.astype(BF16), b.astype(BF16), (((1,), (1,)), ((), ())),
                           preferred_element_type=F32)


def _rms(x, g):
    return x * lax.rsqrt(jnp.mean(x * x, axis=-1, keepdims=True) + EPS) * g


def _inproj_kernel(h_ref, g_ref, w_ref, zhg_ref, zat_ref, zg_ref):
    yb = _rms(h_ref[...], g_ref[...]).astype(BF16)
    c0 = 4 * HG_WIDTH
    c1 = c0 + AT_WIDTH + 2 * AT_KV_WIDTH
    zhg_ref[...] = jnp.dot(yb, w_ref[:, 0:c0], preferred_element_type=F32)
    zat_ref[...] = jnp.dot(yb, w_ref[:, c0:c1], preferred_element_type=F32)
    zg_ref[...] = jnp.dot(yb, w_ref[:, c1:], preferred_element_type=F32)


def _hgrn_kernel(z_ref, lbl_ref, ng_ref, o_ref, st_ref):
    n = pl.program_id(1)

    @pl.when(n == 0)
    def _():
        st_ref[...] = jnp.zeros_like(st_ref)

    W = HG_WIDTH
    q = z_ref[:, 0:W]
    fl = z_ref[:, W:2 * W]
    v = z_ref[:, 2 * W:3 * W]
    gate = z_ref[:, 3 * W:4 * W]

    l = lbl_ref[...]
    e = jnp.exp(l - jnp.max(l, axis=0, keepdims=True))
    lb = e[0:1] / jnp.sum(e, axis=0, keepdims=True)

    f = lb + (1.0 - lb) * jax.nn.sigmoid(fl)
    row = lax.broadcasted_iota(jnp.int32, (BLOCK, W), 0)
    live = jnp.logical_or(n > 0, row >= N_PAD)
    logf = jnp.where(live, jnp.log(f), 0.0)
    k = jnp.where(live, 1.0 - f, 0.0)

    A = logf
    s = 1
    while s < BLOCK:
        A = A + jnp.where(row >= s, pltpu.roll(A, s, axis=0), 0.0)
        s *= 2

    q_lv = [q]
    k_lv = [k]
    piv_q = jnp.where(row >= 1, pltpu.roll(A, 1, axis=0), 0.0)
    piv_k = A
    m = 1
    while m < BLOCK:
        bit = (row & m) != 0
        q_lv.append(q * jnp.exp(A - piv_q))
        k_lv.append(k * jnp.exp(piv_k - A))
        piv_q = jnp.where(bit, pltpu.roll(piv_q, m, axis=0), piv_q)
        piv_k = jnp.where(bit, piv_k, pltpu.roll(piv_k, BLOCK - m, axis=0))
        m *= 2
    q_in = q * jnp.exp(A)
    k_out = k * jnp.exp(piv_k - A)
    a_end = piv_k[0:1, :]

    ti = lax.broadcasted_iota(jnp.int32, (BLOCK, BLOCK), 0)
    si = lax.broadcasted_iota(jnp.int32, (BLOCK, BLOCK), 1)
    tx = ti ^ si
    ng = ng_ref[...]
    for hh in range(HG_HEADS):
        sl = slice(hh * HG_DK, (hh + 1) * HG_DK)
        scores = jnp.where(ti == si, _dot_nt(q_lv[0][:, sl], k_lv[0][:, sl]), 0.0)
        m = 1
        lv = 1
        while m < BLOCK:
            mask = jnp.logical_and((tx >> (lv - 1)) == 1, (ti & m) != 0)
            scores = scores + jnp.where(mask, _dot_nt(q_lv[lv][:, sl], k_lv[lv][:, sl]), 0.0)
            m *= 2
            lv += 1
        st = st_ref[hh]
        o = _dot(scores, v[:, sl]) + _dot_nt(q_in[:, sl], st)
        st_ref[hh] = st * jnp.exp(a_end[:, sl]) + _dot(v[:, sl].T, k_out[:, sl])
        o = _rms(o, ng[:, sl])
        g = gate[:, sl]
        o_ref[:, sl] = o * (g * jax.nn.sigmoid(g))


def _swa_kernel(sink_ref, cur_ref, prev_ref, first_ref, o_ref):
    n = pl.program_id(1)
    R = AT_GROUP * BLOCK
    NK = 3 * BLOCK
    c = lax.broadcasted_iota(jnp.int32, (NK, R), 0)
    r = lax.broadcasted_iota(jnp.int32, (NK, R), 1) & (BLOCK - 1)
    qpos = n * BLOCK + r
    band = (c > r) & (c <= r + BLOCK) & ((n - 1) * BLOCK + c >= BLOCK) & (c < 2 * BLOCK)
    cm = c - 2 * BLOCK
    meta = (cm >= N_PAD) & (cm <= qpos)
    mask = band | meta
    neg = jnp.finfo(F32).min
    scale = AT_HDIM ** -0.5
    ko = AT_WIDTH
    vo = AT_WIDTH + AT_KV_WIDTH
    for bi in range(cur_ref.shape[0]):
        kv_cols = slice(vo, vo + AT_KV_WIDTH)
        v_t = jnp.concatenate([prev_ref[bi, :, kv_cols], cur_ref[bi, :, kv_cols],
                               first_ref[bi, :, kv_cols]], axis=0).T
        outs = []
        for g in range(AT_KVHEADS):
            ks = slice(ko + g * AT_HDIM, ko + (g + 1) * AT_HDIM)
            kall = jnp.concatenate([prev_ref[bi, :, ks], cur_ref[bi, :, ks], first_ref[bi, :, ks]],
                                   axis=0)
            qs = jnp.concatenate(
                [cur_ref[bi, :, (g * AT_GROUP + j) * AT_HDIM:(g * AT_GROUP + j + 1) * AT_HDIM]
                 for j in range(AT_GROUP)], axis=0)
            sink = jnp.concatenate(
                [jnp.full((1, BLOCK), sink_ref[g * AT_GROUP + j], F32) for j in range(AT_GROUP)],
                axis=1)
            sc = jnp.where(mask, _dot_nt(kall, qs) * scale, neg)
            mx = jnp.maximum(jnp.max(sc, axis=0, keepdims=True), sink)
            p = jnp.exp(sc - mx)
            den = jnp.sum(p, axis=0, keepdims=True) + jnp.exp(sink - mx)
            o_t = _dot(v_t[g * AT_HDIM:(g + 1) * AT_HDIM, :], p) / den
            outs += [o_t[:, j * BLOCK:(j + 1) * BLOCK] for j in range(AT_GROUP)]
        o_ref[bi] = jnp.concatenate(outs, axis=0).T


def _merge_kernel(h_ref, ohg_ref, oat_ref, zg_ref, bg_ref, wbh_ref, wba_ref, wo_ref, fg_ref,
                  wq_ref, sk_ref, h1_ref, xb_ref, st_ref):
    D = h_ref.shape[-1]
    gl = zg_ref[...] + bg_ref[...]
    g_hg = jax.nn.sigmoid(gl[:, :D])
    g_at = jax.nn.sigmoid(gl[:, D:])
    mrg = (g_hg * jnp.dot(ohg_ref[...].astype(BF16), wbh_ref[...], preferred_element_type=F32)
           + g_at * jnp.dot(oat_ref[...].astype(BF16), wba_ref[...], preferred_element_type=F32))
    h1 = h_ref[...] + jnp.dot(mrg.astype(BF16), wo_ref[...], preferred_element_type=F32)
    h1_ref[...] = h1
    xn = _rms(h1, fg_ref[...])
    xb = xn.astype(BF16)
    xb_ref[...] = xn.T.astype(BF16)
    qp = jnp.dot(xb, wq_ref[...], preferred_element_type=F32).astype(BF16)
    for hc in range(2 * PK_HEADS):
        st_ref[hc] = lax.dot_general(sk_ref[hc], qp[:, hc * N_KEYS:(hc + 1) * N_KEYS],
                                     (((1,), (1,)), ((), ())), preferred_element_type=F32)


SUBLANES = 8


def _oddeven_mergesort_pairs(n):
    pairs = []
    p = 1
    while p < n:
        k = p
        while k >= 1:
            for j in range(k % p, n - k, 2 * k):
                for i in range(min(k, n - j - k)):
                    if (i + j) // (2 * p) == (i + j + k) // (2 * p):
                        pairs.append((i + j, i + j + k))
            k //= 2
        p *= 2
    return pairs


def _exchange(x, i, j):
    x[i], x[j] = jnp.maximum(x[i], x[j]), jnp.minimum(x[i], x[j])


def _top16_sorted(slabs):
    x = list(slabs)
    for i, j in _oddeven_mergesort_pairs(TOPK):
        _exchange(x, i, j)
    shift = SUBLANES // 2
    while shift >= 1:
        y = [pltpu.roll(v, shift, axis=0) for v in x]
        x = [jnp.maximum(x[k], y[TOPK - 1 - k]) for k in range(TOPK)]
        d = TOPK // 2
        while d >= 1:
            for k in range(TOPK):
                if k & d == 0:
                    _exchange(x, k, k + d)
            d //= 2
        shift //= 2
    return [v[0:1, :] for v in x]


def _key_slabs(s):
    return [s[SUBLANES * g:SUBLANES * (g + 1), :] for g in range(s.shape[0] // SUBLANES)]


def _peer_select(st_ref, c_ref, e0_ref, r1_ref, e1_ref):
    for h in range(PK_HEADS):
        s0 = st_ref[2 * h]
        s1 = st_ref[2 * h + 1]
        a_rows = _top16_sorted(_key_slabs(s0))
        b_rows = _top16_sorted(_key_slabs(s1))
        a = jnp.concatenate(a_rows, axis=0)
        b = jnp.concatenate(b_rows, axis=0)
        half = TOPK // 2
        cand = [a_rows[0] + b[0:half], a_rows[0] + b[half:]]
        cand += [a_rows[ra] + b[0:half] for ra in range(1, half)]
        cand.append(a[half:] + b_rows[0])
        cand += [jnp.full_like(cand[0], MASKED)] * (TOPK - len(cand))
        tau = _top16_sorted(cand)[TOPK - 1]
        ea = jnp.exp(a - a_rows[0])
        eb = jnp.exp(b - b_rows[0])
        z = jnp.zeros_like(tau)
        cnt = jnp.zeros_like(s0)
        for k in range(TOPK):
            hit = a + b_rows[k] >= tau
            z = z + eb[k:k + 1] * jnp.sum(jnp.where(hit, ea, 0.0), axis=0, keepdims=True)
            reach = jnp.min(jnp.where(hit, a, -MASKED), axis=0, keepdims=True)
            cnt = jnp.where(s0 >= reach, float(k + 1), cnt)
        rank1 = jnp.full_like(s1, float(TOPK))
        for k in reversed(range(TOPK)):
            rank1 = jnp.where(s1 >= b_rows[k], float(k), rank1)
        e0 = jnp.exp(s0 - a_rows[0]) * (0.5 / z)
        e1 = jnp.exp(s1 - b_rows[0])
        for q in range(s0.shape[1] // BLOCK):
            lanes = slice(q * BLOCK, (q + 1) * BLOCK)
            c_ref[h, q] = cnt[:, lanes]
            e0_ref[h, q] = e0[:, lanes]
            r1_ref[h, q] = rank1[:, lanes].astype(BF16)
            e1_ref[h, q] = e1[:, lanes].astype(BF16)


def _peer_kernel(st_ref, xbt_ref, h1_ref, u_ref, vt_ref, fg_ref, out_ref,
                 c_ref, e0_ref, r1_ref, e1_ref, acc_ref, crow_ref, e0row_ref,
                 at0_ref, at1_ref, w0_ref, w1_ref):
    e = pl.program_id(1)
    n_e = pl.num_programs(1)
    n_grp = u_ref.shape[0]
    PAIR = 2 * N_KEYS
    pairs_per_grp = u_ref.shape[1] // PAIR
    n_pair = n_grp * pairs_per_grp
    TB = xbt_ref.shape[1]

    @pl.when(e == 0)
    def _():
        acc_ref[...] = jnp.zeros_like(acc_ref)
        _peer_select(st_ref, c_ref, e0_ref, r1_ref, e1_ref)

    n_row = 2 * n_pair
    n_q = TB // BLOCK
    base = pl.multiple_of(e * n_row, n_row)
    for h in range(PK_HEADS):
        for q in range(n_q):
            crow_ref[h, q] = c_ref[h, q, pl.ds(base, n_row), :]
            e0row_ref[h, q] = e0_ref[h, q, pl.ds(base, n_row), :]

    def pre_act(g, dst_ref):
        res = jnp.dot(u_ref[g], xbt_ref[...], preferred_element_type=F32)
        for q in range(n_q):
            dst_ref[q] = res[:, q * BLOCK:(q + 1) * BLOCK]

    def weights(p, src_ref, dst_ref):
        r0 = (p % pairs_per_grp) * PAIR
        for q in range(n_q):
            lanes = slice(q * BLOCK, (q + 1) * BLOCK)
            gates = [jnp.zeros((N_KEYS, BLOCK), BF16) for _ in range(2)]
            for h in range(PK_HEADS):
                r1 = r1_ref[h, q]
                e1 = e1_ref[h, q]
                for d in range(2):
                    i = 2 * p + d
                    cnt = crow_ref[h, q, i:i + 1, :].astype(BF16)
                    e0 = e0row_ref[h, q, i:i + 1, :].astype(BF16)
                    gates[d] = gates[d] + jnp.where(r1 < cnt, e1, jnp.zeros((), BF16)) * e0
            for d in range(2):
                rows = slice(r0 + d * N_KEYS, r0 + (d + 1) * N_KEYS)
                a = src_ref[q, rows, :]
                act = a * (1.0 + lax.erf(a * math.sqrt(0.5)))
                dst_ref[q, rows, :] = act.astype(BF16) * gates[d]

    def expert_out(g, src_ref):
        w = jnp.concatenate([src_ref[q] for q in range(n_q)], axis=1)
        acc_ref[...] += jnp.dot(vt_ref[g], w, preferred_element_type=F32)

    at_refs = (at0_ref, at1_ref)
    w_refs = (w0_ref, w1_ref)
    pre_act(0, at_refs[0])
    for g in range(n_grp):
        if g + 1 < n_grp:
            pre_act(g + 1, at_refs[(g + 1) % 2])
        for k in range(pairs_per_grp):
            weights(g * pairs_per_grp + k, at_refs[g % 2], w_refs[g % 2])
        if g > 0:
            expert_out(g - 1, w_refs[(g - 1) % 2])
    expert_out(n_grp - 1, w_refs[(n_grp - 1) % 2])

    @pl.when(e == n_e - 1)
    def _():
        out_ref[...] = _rms(h1_ref[...] + acc_ref[...].T, fg_ref[...])


def _tile(total, pref):
    t = pref
    while total % t:
        t //= 2
    return t


def _const_spec(shape):
    return pl.BlockSpec(shape, lambda *_: (0,) * len(shape))


def kernel(x, meta_tokens, mix_norm_g, w_in, b_gate, hg_lb_logits, hg_norm_g, attn_sinks,
           w_branch_hg, w_branch_attn, w_out, ffn_norm_g, peer_w_q, peer_subkeys,
           peer_u, peer_v, final_norm_g):
    B, S, D = x.shape
    assert S % BLOCK == 0 and w_in.shape[0] == 1
    P = S + BLOCK
    NB = P // BLOCK
    T = B * P
    n_exp = peer_u.shape[1]

    h0 = jnp.concatenate(
        [jnp.zeros((B, N_PAD, D), x.dtype),
         jnp.broadcast_to(meta_tokens[None].astype(x.dtype), (B, N_META, D)), x],
        axis=1).reshape(T, D)

    w_in_b = w_in[0].astype(BF16)
    in_cols = w_in_b.shape[1]
    c_hg = 4 * HG_WIDTH
    c_at = AT_WIDTH + 2 * AT_KV_WIDTH
    c_g = 2 * D

    TM = _tile(T, 256)
    z_hg, z_at, z_g = pl.pallas_call(
        _inproj_kernel,
        grid=(T // TM,),
        in_specs=[pl.BlockSpec((TM, D), lambda i: (i, 0)),
                  _const_spec((1, D)),
                  _const_spec((D, in_cols))],
        out_specs=[pl.BlockSpec((TM, c_hg), lambda i: (i, 0)),
                   pl.BlockSpec((TM, c_at), lambda i: (i, 0)),
                   pl.BlockSpec((TM, c_g), lambda i: (i, 0))],
        out_shape=[jax.ShapeDtypeStruct((T, c_hg), F32),
                   jax.ShapeDtypeStruct((T, c_at), F32),
                   jax.ShapeDtypeStruct((T, c_g), F32)],
        compiler_params=pltpu.CompilerParams(dimension_semantics=("parallel",),
                                             vmem_limit_bytes=VMEM_LIMIT),
        name="inproj",
    )(h0, mix_norm_g[0:1], w_in_b)

    o_hg = pl.pallas_call(
        _hgrn_kernel,
        grid=(B, NB),
        in_specs=[pl.BlockSpec((BLOCK, c_hg), lambda b, n: (b * NB + n, 0)),
                  _const_spec((2, HG_WIDTH)),
                  _const_spec((1, HG_WIDTH))],
        out_specs=pl.BlockSpec((BLOCK, HG_WIDTH), lambda b, n: (b * NB + n, 0)),
        out_shape=jax.ShapeDtypeStruct((T, HG_WIDTH), F32),
        scratch_shapes=[pltpu.VMEM((HG_HEADS, HG_DK, HG_DK), F32)],
        compiler_params=pltpu.CompilerParams(dimension_semantics=("parallel", "arbitrary"),
                                             vmem_limit_bytes=VMEM_LIMIT),
        name="hgrn",
    )(z_hg, hg_lb_logits.astype(F32), hg_norm_g[0:1])

    BS = 4 if B % 4 == 0 else 1
    z_at3 = z_at.reshape(B, P, c_at)
    o_at = pl.pallas_call(
        _swa_kernel,
        grid=(B // BS, NB),
        in_specs=[pl.BlockSpec(memory_space=pltpu.SMEM),
                  pl.BlockSpec((BS, BLOCK, c_at), lambda b, n: (b, n, 0)),
                  pl.BlockSpec((BS, BLOCK, c_at), lambda b, n: (b, jnp.maximum(n - 1, 0), 0)),
                  pl.BlockSpec((BS, BLOCK, c_at), lambda b, n: (b, 0, 0))],
        out_specs=pl.BlockSpec((BS, BLOCK, AT_WIDTH), lambda b, n: (b, n, 0)),
        out_shape=jax.ShapeDtypeStruct((B, P, AT_WIDTH), F32),
        compiler_params=pltpu.CompilerParams(dimension_semantics=("parallel", "arbitrary"),
                                             vmem_limit_bytes=VMEM_LIMIT),
        name="swa",
    )(attn_sinks[0].astype(F32), z_at3, z_at3, z_at3).reshape(T, AT_WIDTH)

    n_hc = 2 * PK_HEADS
    sk = peer_subkeys[0].reshape(n_hc, N_KEYS, -1).astype(BF16)
    d_half = sk.shape[-1]
    TM4 = _tile(T, 256)
    h1, xbt, st = pl.pallas_call(
        _merge_kernel,
        grid=(T // TM4,),
        in_specs=[pl.BlockSpec((TM4, D), lambda i: (i, 0)),
                  pl.BlockSpec((TM4, HG_WIDTH), lambda i: (i, 0)),
                  pl.BlockSpec((TM4, AT_WIDTH), lambda i: (i, 0)),
                  pl.BlockSpec((TM4, c_g), lambda i: (i, 0)),
                  _const_spec((1, c_g)),
                  _const_spec((HG_WIDTH, D)),
                  _const_spec((AT_WIDTH, D)),
                  _const_spec((D, D)),
                  _const_spec((1, D)),
                  _const_spec((D, n_hc * d_half)),
                  _const_spec((n_hc, N_KEYS, d_half))],
        out_specs=[pl.BlockSpec((TM4, D), lambda i: (i, 0)),
                   pl.BlockSpec((D, TM4), lambda i: (0, i)),
                   pl.BlockSpec((n_hc, N_KEYS, TM4), lambda i: (0, 0, i))],
        out_shape=[jax.ShapeDtypeStruct((T, D), F32),
                   jax.ShapeDtypeStruct((D, T), BF16),
                   jax.ShapeDtypeStruct((n_hc, N_KEYS, T), F32)],
        compiler_params=pltpu.CompilerParams(dimension_semantics=("parallel",),
                                             vmem_limit_bytes=VMEM_LIMIT),
        name="merge",
    )(h0, o_hg, o_at, z_g, b_gate[0:1], w_branch_hg[0].astype(BF16),
      w_branch_attn[0].astype(BF16), w_out[0].astype(BF16), ffn_norm_g[0:1],
      peer_w_q[0].astype(BF16), sk)

    TB = _tile(T, 512)
    PAIR = 2 * N_KEYS
    PB = 8
    GRP = 2 * PAIR
    n_grps = n_exp // GRP
    GB = PB * PAIR // GRP
    u_b = peer_u[0].astype(BF16).reshape(n_grps, GRP, D)
    vt_b = peer_v[0].astype(BF16).reshape(n_grps, GRP, D).transpose(0, 2, 1)
    out = pl.pallas_call(
        _peer_kernel,
        grid=(T // TB, n_grps // GB),
        in_specs=[pl.BlockSpec((n_hc, N_KEYS, TB), lambda i, e: (0, 0, i)),
                  pl.BlockSpec((D, TB), lambda i, e: (0, i)),
                  pl.BlockSpec((TB, D), lambda i, e: (i, 0)),
                  pl.BlockSpec((GB, GRP, D), lambda i, e: (e, 0, 0)),
                  pl.BlockSpec((GB, D, GRP), lambda i, e: (e, 0, 0)),
                  _const_spec((1, D))],
        out_specs=pl.BlockSpec((TB, D), lambda i, e: (i, 0)),
        out_shape=jax.ShapeDtypeStruct((T, D), F32),
        scratch_shapes=[pltpu.VMEM((PK_HEADS, TB // BLOCK, N_KEYS, BLOCK), F32),
                        pltpu.VMEM((PK_HEADS, TB // BLOCK, N_KEYS, BLOCK), F32),
                        pltpu.VMEM((PK_HEADS, TB // BLOCK, N_KEYS, BLOCK), BF16),
                        pltpu.VMEM((PK_HEADS, TB // BLOCK, N_KEYS, BLOCK), BF16),
                        pltpu.VMEM((D, TB), F32),
                        pltpu.VMEM((PK_HEADS, TB // BLOCK, 2 * PB, BLOCK), F32),
                        pltpu.VMEM((PK_HEADS, TB // BLOCK, 2 * PB, BLOCK), F32),
                        pltpu.VMEM((TB // BLOCK, GRP, BLOCK), F32),
                        pltpu.VMEM((TB // BLOCK, GRP, BLOCK), F32),
                        pltpu.VMEM((TB // BLOCK, GRP, BLOCK), BF16),
                        pltpu.VMEM((TB // BLOCK, GRP, BLOCK), BF16)],
        compiler_params=pltpu.CompilerParams(dimension_semantics=("parallel", "arbitrary"),
                                             vmem_limit_bytes=56 * 1024 * 1024),
        name="peer",
    )(st, xbt, h1, u_b, vt_b, final_norm_g.reshape(1, D))

    return out.reshape(B, P, D)[:, BLOCK:]
```
